```python
import math
import jax, jax.numpy as jnp
from jax import lax
import numpy as np

D_MODEL = 1024
BATCH = 16
SEQ = 2048
DEPTH = 2

N_ATTN_HEADS = 4
QK_NOPE_DIM = 128
QK_ROPE_DIM = 64
QK_HEAD_DIM = QK_NOPE_DIM + QK_ROPE_DIM
V_HEAD_DIM = 128
Q_RANK = 256
KV_RANK = 128
ATTN_WIDTH = N_ATTN_HEADS * V_HEAD_DIM
ROPE_THETA = 10000.0
Q_BLOCK = 128
CONV_CH = 512
CONV_WIDTH = 31
CONV_PAD = CONV_WIDTH // 2
D_MIX = ATTN_WIDTH + CONV_CH
IN_WIDTH = Q_RANK + KV_RANK + QK_ROPE_DIM + 2 * CONV_CH
N_EXPERTS = 16
CAPACITY_FACTOR = 2
EXPERT_FF = 1024
N_MOD = 6
EPS = 1e-6

kernel_name = "hybrid_mla_conformer_ecmoe_encoder"


def rms_norm(x, g):
    xf = x.astype(jnp.float32)
    y = xf * lax.rsqrt(jnp.mean(xf * xf, axis=-1, keepdims=True) + EPS)
    return (y * g.astype(jnp.float32)).astype(x.dtype)


def layer_norm(x, g, b):
    xf = x.astype(jnp.float32)
    mu = jnp.mean(xf, axis=-1, keepdims=True)
    var = jnp.mean(jnp.square(xf - mu), axis=-1, keepdims=True)
    y = (xf - mu) * lax.rsqrt(var + EPS)
    return (y * g.astype(jnp.float32) + b.astype(jnp.float32)).astype(x.dtype)


def rope_tables(positions):
    inv_freq = 1.0 / (ROPE_THETA ** (jnp.arange(0, QK_ROPE_DIM, 2, dtype=jnp.float32) / QK_ROPE_DIM))
    ang = positions.astype(jnp.float32)[..., None] * inv_freq
    return jnp.cos(ang), jnp.sin(ang)


def apply_rope(x, cos, sin):
    half = x.shape[-1] // 2
    x1 = x[..., :half].astype(jnp.float32)
    x2 = x[..., half:].astype(jnp.float32)
    c = cos[:, :, None, :]
    s = sin[:, :, None, :]
    return jnp.concatenate([x1 * c - x2 * s, x2 * c + x1 * s], axis=-1).astype(x.dtype)


def mla_group(c_q, c_kv, k_rope, cos, sin, q_latent_g, w_uq, kv_latent_g, w_ukv, q_head_g, k_head_g):
    B, S, _ = c_q.shape
    q = jnp.einsum('bsr,rk->bsk', rms_norm(c_q, q_latent_g), w_uq).reshape(B, S, N_ATTN_HEADS, QK_HEAD_DIM)
    kv = jnp.einsum('bsr,rk->bsk', rms_norm(c_kv, kv_latent_g), w_ukv).reshape(
        B, S, N_ATTN_HEADS, QK_NOPE_DIM + V_HEAD_DIM)
    k_nope, v = kv[..., :QK_NOPE_DIM], kv[..., QK_NOPE_DIM:]
    k_r = jnp.broadcast_to(k_rope[:, :, None, :], (B, S, N_ATTN_HEADS, QK_ROPE_DIM))
    k = jnp.concatenate([k_nope, k_r], axis=-1)
    q = rms_norm(q, q_head_g)
    k = rms_norm(k, k_head_g)
    q = jnp.concatenate([q[..., :QK_NOPE_DIM], apply_rope(q[..., QK_NOPE_DIM:], cos, sin)], axis=-1)
    k = jnp.concatenate([k[..., :QK_NOPE_DIM], apply_rope(k[..., QK_NOPE_DIM:], cos, sin)], axis=-1)
    scale = 1.0 / math.sqrt(QK_HEAD_DIM)
    n_blocks = S // Q_BLOCK
    qb = q.reshape(B, n_blocks, Q_BLOCK, N_ATTN_HEADS, QK_HEAD_DIM).transpose(1, 0, 3, 2, 4)
    kt = k.transpose(0, 2, 1, 3)
    vt = v.transpose(0, 2, 1, 3)

    def attend(q_blk):
        s = jnp.einsum('bhqd,bhkd->bhqk', q_blk, kt, preferred_element_type=jnp.float32) * scale
        p = jax.nn.softmax(s, axis=-1).astype(vt.dtype)
        return jnp.einsum('bhqk,bhkd->bhqd', p, vt)

    o = lax.map(attend, qb)
    return o.transpose(1, 0, 3, 2, 4).reshape(B, S, ATTN_WIDTH)


def conv_group(u, conv_w, conv_b, conv_norm_g, conv_norm_b):
    a, g = u[..., :CONV_CH], u[..., CONV_CH:]
    y = a * jax.nn.sigmoid(g)
    y = lax.conv_general_dilated(
        y, conv_w[:, None, :].astype(y.dtype), window_strides=(1,), padding=[(CONV_PAD, CONV_PAD)],
        dimension_numbers=('NWC', 'WIO', 'NWC'), feature_group_count=CONV_CH)
    y = y + conv_b
    return jax.nn.silu(layer_norm(y, conv_norm_g, conv_norm_b))


def expert_choice_ffn(h, w_router, w_gate, w_up, w_down):
    B, S, D = h.shape
    cap = CAPACITY_FACTOR * S // N_EXPERTS
    logits = jnp.einsum('bsd,de->bse', h, w_router).astype(jnp.float32)
    aff = jax.nn.softmax(logits, axis=-1)
    gate, idx = lax.top_k(aff.transpose(0, 2, 1), cap)
    xe = jax.vmap(lambda hb, ib: hb[ib])(h, idx)
    hid = jax.nn.silu(jnp.einsum('becd,edf->becf', xe, w_gate)) * jnp.einsum('becd,edf->becf', xe, w_up)
    ye = jnp.einsum('becf,efd->becd', hid, w_down) * gate[..., None].astype(h.dtype)
    return jax.vmap(lambda ib, yb: jnp.zeros((S, D), yb.dtype).at[ib.reshape(-1)].add(yb.reshape(-1, D)))(idx, ye)


def setup_inputs(seed: int = 0) -> dict:
    key = jax.random.key(seed)
    ks = jax.random.split(key, 24)
    f32 = jnp.float32
    nrm = lambda k, shape, s: jax.random.normal(k, shape, f32) * s
    gain = lambda k, shape: 1.0 + 0.05 * jax.random.normal(k, shape, f32)
    L = DEPTH
    offset = jax.random.randint(ks[2], (BATCH, 1), 0, 1024, dtype=jnp.int32)
    positions = offset + jnp.arange(SEQ, dtype=jnp.int32)[None, :]
    return {
        "x": nrm(ks[0], (BATCH, SEQ, D_MODEL), 1.0),
        "c": nrm(ks[1], (BATCH, D_MODEL), 1.0),
        "positions": positions,
        "norm1_g": gain(ks[3], (L, D_MODEL)),
        "w_ada": nrm(ks[4], (L, D_MODEL, N_MOD * D_MODEL), 0.5 * D_MODEL ** -0.5),
        "b_ada": nrm(ks[5], (L, N_MOD * D_MODEL), 0.01),
        "w_in": nrm(ks[6], (L, D_MODEL, IN_WIDTH), D_MODEL ** -0.5),
        "q_latent_g": gain(ks[7], (L, Q_RANK)),
        "w_uq": nrm(ks[8], (L, Q_RANK, N_ATTN_HEADS * QK_HEAD_DIM), Q_RANK ** -0.5),
        "kv_latent_g": gain(ks[9], (L, KV_RANK)),
        "w_ukv": nrm(ks[10], (L, KV_RANK, N_ATTN_HEADS * (QK_NOPE_DIM + V_HEAD_DIM)), KV_RANK ** -0.5),
        "q_head_g": gain(ks[11], (L, QK_HEAD_DIM)),
        "k_head_g": gain(ks[12], (L, QK_HEAD_DIM)),
        "conv_w": nrm(ks[13], (L, CONV_WIDTH, CONV_CH), CONV_WIDTH ** -0.5),
        "conv_b": nrm(ks[14], (L, CONV_CH), 0.01),
        "conv_norm_g": gain(ks[15], (L, CONV_CH)),
        "conv_norm_b": nrm(ks[16], (L, CONV_CH), 0.01),
        "w_out": nrm(ks[17], (L, D_MIX, D_MODEL), D_MIX ** -0.5),
        "norm2_g": gain(ks[18], (L, D_MODEL)),
        "w_router": nrm(ks[19], (L, D_MODEL, N_EXPERTS), D_MODEL ** -0.5),
        "w_gate": nrm(ks[20], (L, N_EXPERTS, D_MODEL, EXPERT_FF), D_MODEL ** -0.5),
        "w_up": nrm(ks[21], (L, N_EXPERTS, D_MODEL, EXPERT_FF), D_MODEL ** -0.5),
        "w_down": nrm(ks[22], (L, N_EXPERTS, EXPERT_FF, D_MODEL), EXPERT_FF ** -0.5),
    }


def reference(x, c, positions, norm1_g, w_ada, b_ada, w_in, q_latent_g, w_uq, kv_latent_g, w_ukv,
              q_head_g, k_head_g, conv_w, conv_b, conv_norm_g, conv_norm_b, w_out, norm2_g,
              w_router, w_gate, w_up, w_down):
    cos, sin = rope_tables(positions)
    c_act = jax.nn.silu(c)
    o1 = Q_RANK
    o2 = o1 + KV_RANK
    o3 = o2 + QK_ROPE_DIM
    for l in range(DEPTH):
        mod = jnp.einsum('bd,dk->bk', c_act, w_ada[l]) + b_ada[l]
        shift1, scale1, gate1, shift2, scale2, gate2 = [m[:, None, :] for m in jnp.split(mod, N_MOD, axis=-1)]
        h = rms_norm(x, norm1_g[l]) * (1.0 + scale1) + shift1
        proj = jnp.einsum('bsd,dk->bsk', h, w_in[l])
        c_q, c_kv = proj[..., :o1], proj[..., o1:o2]
        k_rope = proj[..., o2:o3]
        conv_u = proj[..., o3:]
        attn_out = mla_group(c_q, c_kv, k_rope, cos, sin, q_latent_g[l], w_uq[l], kv_latent_g[l], w_ukv[l],
                             q_head_g[l], k_head_g[l])
        conv_out = conv_group(conv_u, conv_w[l], conv_b[l], conv_norm_g[l], conv_norm_b[l])
        mix = jnp.einsum('bsk,kd->bsd', jnp.concatenate([attn_out, conv_out], axis=-1), w_out[l])
        x = x + gate1 * mix
        h2 = rms_norm(x, norm2_g[l]) * (1.0 + scale2) + shift2
        x = x + gate2 * expert_choice_ffn(h2, w_router[l], w_gate[l], w_up[l], w_down[l])
    return x
```

```python
import functools
import math

import jax
import jax.numpy as jnp
from jax import lax
from jax.experimental import pallas as pl
from jax.experimental.pallas import tpu as pltpu

F32 = jnp.float32
BF16 = jnp.bfloat16
HIGHEST = lax.Precision.HIGHEST

D_MODEL = 1024
N_HEADS = 4
NOPE = 128
ROPE = 64
QK_DIM = NOPE + ROPE
V_DIM = 128
Q_RANK = 256
KV_RANK = 128
CONV_CH = 512
CONV_WIDTH = 31
CONV_PAD = CONV_WIDTH // 2
N_EXPERTS = 16
EXPERT_FF = 1024
N_MOD = 6
ROPE_THETA = 10000.0
EPS = 1e-6

LANES = 128
HALO = 16
MIB = 1024 * 1024


def _cparams(sem, vmem_mib):
    return pltpu.CompilerParams(dimension_semantics=sem, vmem_limit_bytes=vmem_mib * MIB)


def _ada_kernel(c_ref, w_ref, b_ref, o_ref):
    c = c_ref[...]
    ca = c * jax.nn.sigmoid(c)
    o_ref[0] = jnp.dot(ca, w_ref[0], precision=HIGHEST, preferred_element_type=F32) + b_ref[0]


def _ada_call(c, w_ada, b_ada):
    depth, d, n = w_ada.shape
    b = c.shape[0]
    tn = 1536
    return pl.pallas_call(
        _ada_kernel,
        grid=(depth, n // tn),
        in_specs=[
            pl.BlockSpec((b, d), lambda l, j: (0, 0)),
            pl.BlockSpec((1, d, tn), lambda l, j: (l, 0, j)),
            pl.BlockSpec((1, 1, tn), lambda l, j: (l, 0, j)),
        ],
        out_specs=pl.BlockSpec((1, b, tn), lambda l, j: (l, 0, j)),
        out_shape=jax.ShapeDtypeStruct((depth, b, n), F32),
        compiler_params=_cparams(("parallel", "parallel"), 40),
        name="adaln_mod",
    )(c, w_ada, b_ada.reshape(depth, 1, n))


def _rope_kernel(pos_ref, invf_ref, cos_ref, sin_ref):
    ang = pos_ref[0].astype(F32) * invf_ref[...]
    cos_ref[0] = jnp.cos(ang)
    sin_ref[0] = jnp.sin(ang)


def _rope_tables(positions):
    b, s = positions.shape
    half = ROPE // 2
    per_row = LANES // half
    inv_freq = 1.0 / (ROPE_THETA ** (jnp.arange(0, ROPE, 2, dtype=F32) / ROPE))
    invf = jnp.tile(inv_freq, per_row).reshape(1, LANES)
    pos_rep = jnp.repeat(positions.reshape(b, s // per_row, per_row), half, axis=-1)
    cos_p, sin_p = pl.pallas_call(
        _rope_kernel,
        grid=(b,),
        in_specs=[
            pl.BlockSpec((1, s // per_row, LANES), lambda i: (i, 0, 0)),
            pl.BlockSpec((1, LANES), lambda i: (0, 0)),
        ],
        out_specs=[pl.BlockSpec((1, s // per_row, LANES), lambda i: (i, 0, 0))] * 2,
        out_shape=[jax.ShapeDtypeStruct((b, s // per_row, LANES), F32)] * 2,
        compiler_params=_cparams(("parallel",), 32),
        name="rope_tables",
    )(pos_rep, invf)
    cos = cos_p.reshape(b, s, half)
    sin = sin_p.reshape(b, s, half)
    cos4 = jnp.concatenate([cos, cos, cos, cos], axis=-1)
    sin4 = jnp.concatenate([-sin, sin, -sin, sin], axis=-1)
    return cos4, sin4


def _inproj_kernel(x_ref, mod_ref, g1_ref, win_ref, qlg_ref, wuq_ref, kvlg_ref, wukv_ref,
                   gqn_ref, gqr_ref, gkn_ref, gkr_ref, cos_ref, sin_ref,
                   q_ref, k_ref, v_ref, y_ref):
    x = x_ref[0]
    tm = x.shape[0]
    shift1 = mod_ref[0, 0:1, :]
    scale1 = mod_ref[0, 1:2, :]
    h = x * lax.rsqrt(jnp.mean(x * x, axis=-1, keepdims=True) + EPS) * g1_ref[...]
    h = h * (1.0 + scale1) + shift1
    proj = jnp.dot(h.astype(BF16), win_ref[...], preferred_element_type=F32)
    o1 = Q_RANK
    o2 = o1 + KV_RANK
    o3 = o2 + CONV_CH
    o4 = o3 + CONV_CH
    c_q = proj[:, :o1]
    c_kv = proj[:, o1:o2]
    y_ref[0] = proj[:, o2:o3] * jax.nn.sigmoid(proj[:, o3:o4])
    kr = proj[:, o4:]

    cqn = c_q * lax.rsqrt(jnp.mean(c_q * c_q, axis=-1, keepdims=True) + EPS) * qlg_ref[...]
    qf = jnp.dot(cqn.astype(BF16), wuq_ref[...], preferred_element_type=F32)
    ckn = c_kv * lax.rsqrt(jnp.mean(c_kv * c_kv, axis=-1, keepdims=True) + EPS) * kvlg_ref[...]
    kvf = jnp.dot(ckn.astype(BF16), wukv_ref[...], preferred_element_type=F32)

    cos4 = cos_ref[0]
    sin4 = sin_ref[0]
    lane = lax.broadcasted_iota(jnp.int32, (tm, LANES), 1)
    first_half = (lane & (ROPE // 2)) == 0

    def rope_pair(xb):
        swapped = jnp.where(first_half, pltpu.roll(xb, LANES - ROPE // 2, 1), pltpu.roll(xb, ROPE // 2, 1))
        return xb * cos4 + swapped * sin4

    kr_rot = rope_pair(jnp.concatenate([kr, kr], axis=-1) * gkr_ref[...])[:, :ROPE]
    kr_ss = jnp.sum(kr * kr, axis=-1, keepdims=True)
    nn = N_HEADS * NOPE
    for hh in range(N_HEADS):
        kn = kvf[:, hh * NOPE:(hh + 1) * NOPE]
        ss = jnp.sum(kn * kn, axis=-1, keepdims=True) + kr_ss
        r = lax.rsqrt(ss / QK_DIM + EPS)
        k_ref[0, hh, :, :NOPE] = (kn * r * gkn_ref[...]).astype(BF16)
        k_ref[0, hh, :, NOPE:] = (kr_rot * r).astype(BF16)
        v_ref[0, hh] = kvf[:, nn + hh * V_DIM:nn + (hh + 1) * V_DIM].astype(BF16)

    qscale = 1.0 / math.sqrt(QK_DIM)
    for pair in range(N_HEADS // 2):
        qr = qf[:, nn + pair * LANES:nn + (pair + 1) * LANES]
        qr_rot = rope_pair(qr * gqr_ref[...])
        qr_sq = qr * qr
        for j in range(2):
            hh = 2 * pair + j
            qn = qf[:, hh * NOPE:(hh + 1) * NOPE]
            ss = (jnp.sum(qn * qn, axis=-1, keepdims=True)
                  + jnp.sum(qr_sq[:, j * ROPE:(j + 1) * ROPE], axis=-1, keepdims=True))
            r = lax.rsqrt(ss / QK_DIM + EPS) * qscale
            q_ref[0, hh, :, :NOPE] = (qn * r * gqn_ref[...]).astype(BF16)
            q_ref[0, hh, :, NOPE:] = (qr_rot[:, j * ROPE:(j + 1) * ROPE] * r).astype(BF16)


def _inproj_call(x, mod_l, g1, w_in_p, qlg, w_uq_p, kvlg, w_ukv_p, gqn, gqr2, gkn, gkr2, cos4, sin4, tm=512):
    b, s, d = x.shape
    full = lambda shape: pl.BlockSpec(shape, lambda i, j: (0,) * len(shape))
    return pl.pallas_call(
        _inproj_kernel,
        grid=(b, s // tm),
        in_specs=[
            pl.BlockSpec((1, tm, d), lambda i, j: (i, j, 0)),
            pl.BlockSpec((1, N_MOD, d), lambda i, j: (i, 0, 0)),
            full((1, d)),
            full(w_in_p.shape),
            full((1, Q_RANK)),
            full(w_uq_p.shape),
            full((1, KV_RANK)),
            full(w_ukv_p.shape),
            full((1, NOPE)),
            full((1, LANES)),
            full((1, NOPE)),
            full((1, LANES)),
            pl.BlockSpec((1, tm, LANES), lambda i, j: (i, j, 0)),
            pl.BlockSpec((1, tm, LANES), lambda i, j: (i, j, 0)),
        ],
        out_specs=[
            pl.BlockSpec((1, N_HEADS, tm, QK_DIM), lambda i, j: (i, 0, j, 0)),
            pl.BlockSpec((1, N_HEADS, tm, QK_DIM), lambda i, j: (i, 0, j, 0)),
            pl.BlockSpec((1, N_HEADS, tm, V_DIM), lambda i, j: (i, 0, j, 0)),
            pl.BlockSpec((1, tm, CONV_CH), lambda i, j: (i, j, 0)),
        ],
        out_shape=[
            jax.ShapeDtypeStruct((b, N_HEADS, s, QK_DIM), BF16),
            jax.ShapeDtypeStruct((b, N_HEADS, s, QK_DIM), BF16),
            jax.ShapeDtypeStruct((b, N_HEADS, s, V_DIM), BF16),
            jax.ShapeDtypeStruct((b, s, CONV_CH), F32),
        ],
        compiler_params=_cparams(("parallel", "parallel"), 48),
        name="in_proj",
    )(x, mod_l, g1, w_in_p, qlg, w_uq_p, kvlg, w_ukv_p, gqn, gqr2, gkn, gkr2, cos4, sin4)


def _attn_kernel(q_ref, k_ref, v_ref, o_ref):
    s = lax.dot_general(q_ref[0, 0], k_ref[0, 0], (((1,), (1,)), ((), ())), preferred_element_type=F32)
    m = jnp.max(s, axis=-1, keepdims=True)
    p = jnp.exp(s - m)
    l = jnp.sum(p, axis=-1, keepdims=True)
    o = jnp.dot(p.astype(BF16), v_ref[0, 0], preferred_element_type=F32)
    o_ref[0] = (o / l).astype(BF16)


def _attn_call(q, k, v, tq=512):
    b, nh, s, dk = q.shape
    dv = v.shape[-1]
    return pl.pallas_call(
        _attn_kernel,
        grid=(b, nh, s // tq),
        in_specs=[
            pl.BlockSpec((1, 1, tq, dk), lambda i, h, j: (i, h, j, 0)),
            pl.BlockSpec((1, 1, s, dk), lambda i, h, j: (i, h, 0, 0)),
            pl.BlockSpec((1, 1, s, dv), lambda i, h, j: (i, h, 0, 0)),
        ],
        out_specs=pl.BlockSpec((1, tq, dv), lambda i, h, j: (i, j, h)),
        out_shape=jax.ShapeDtypeStruct((b, s, nh * dv), BF16),
        compiler_params=_cparams(("parallel", "parallel", "parallel"), 48),
        name="attention",
    )(q, k, v)


def _conv_kernel(yc_ref, yp_ref, yn_ref, w_ref, cb_ref, lng_ref, lnb_ref, o_ref, win_ref, *, rc, rs):
    i = pl.program_id(1)
    nblk = pl.num_programs(1)
    win_ref[0:HALO, :] = jnp.where(i > 0, yp_ref[0], 0.0)
    win_ref[HALO:HALO + rc, :] = yc_ref[0]
    win_ref[HALO + rc:HALO + rc + HALO, :] = jnp.where(i < nblk - 1, yn_ref[0], 0.0)
    nch = CONV_CH // LANES
    for sub in range(rc // rs):
        r0 = sub * rs
        accs = []
        for c in range(nch):
            cs = slice(c * LANES, (c + 1) * LANES)
            acc = jnp.zeros((rs, LANES), F32)
            for kk in range(CONV_WIDTH):
                base = HALO - CONV_PAD + r0 + kk
                acc = acc + win_ref[base:base + rs, cs] * w_ref[kk:kk + 1, cs]
            accs.append(acc + cb_ref[:, cs])
        tot = accs[0]
        for c in range(1, nch):
            tot = tot + accs[c]
        mu = jnp.sum(tot, axis=-1, keepdims=True) / CONV_CH
        sq = jnp.zeros((rs, LANES), F32)
        for c in range(nch):
            dlt = accs[c] - mu
            sq = sq + dlt * dlt
        var = jnp.sum(sq, axis=-1, keepdims=True) / CONV_CH
        rstd = lax.rsqrt(var + EPS)
        for c in range(nch):
            cs = slice(c * LANES, (c + 1) * LANES)
            yn = (accs[c] - mu) * rstd * lng_ref[:, cs] + lnb_ref[:, cs]
            o_ref[0, r0:r0 + rs, cs] = (yn * jax.nn.sigmoid(yn)).astype(BF16)


def _conv_call(y, conv_w, conv_b, ln_g, ln_b, rc=256, rs=64):
    b, s, ch = y.shape
    hb = rc // HALO
    nh = s // HALO
    return pl.pallas_call(
        functools.partial(_conv_kernel, rc=rc, rs=rs),
        grid=(b, s // rc),
        in_specs=[
            pl.BlockSpec((1, rc, ch), lambda i, j: (i, j, 0)),
            pl.BlockSpec((1, HALO, ch), lambda i, j: (i, jnp.maximum(j * hb - 1, 0), 0)),
            pl.BlockSpec((1, HALO, ch), lambda i, j: (i, jnp.minimum((j + 1) * hb, nh - 1), 0)),
            pl.BlockSpec((CONV_WIDTH, ch), lambda i, j: (0, 0)),
            pl.BlockSpec((1, ch), lambda i, j: (0, 0)),
            pl.BlockSpec((1, ch), lambda i, j: (0, 0)),
            pl.BlockSpec((1, ch), lambda i, j: (0, 0)),
        ],
        out_specs=pl.BlockSpec((1, rc, ch), lambda i, j: (i, j, 0)),
        out_shape=jax.ShapeDtypeStruct((b, s, ch), BF16),
        scratch_shapes=[pltpu.VMEM((rc + 2 * HALO, ch), F32)],
        compiler_params=_cparams(("parallel", "parallel"), 32),
        name="conv_group",
    )(y, y, y, conv_w, conv_b, ln_g, ln_b)


def _outproj_kernel(a_ref, cv_ref, x_ref, mod_ref, wo_ref, g2_ref, wrt_ref, x1_ref, h2_ref, lg_ref):
    half = a_ref.shape[-1]
    mix = (jnp.dot(a_ref[0], wo_ref[:half, :], preferred_element_type=F32)
           + jnp.dot(cv_ref[0], wo_ref[half:, :], preferred_element_type=F32))
    gate1 = mod_ref[0, 2:3, :]
    shift2 = mod_ref[0, 3:4, :]
    scale2 = mod_ref[0, 4:5, :]
    x1 = x_ref[0] + gate1 * mix
    x1_ref[0] = x1
    h2 = x1 * lax.rsqrt(jnp.mean(x1 * x1, axis=-1, keepdims=True) + EPS) * g2_ref[...]
    h2 = h2 * (1.0 + scale2) + shift2
    h2_ref[0] = h2.astype(BF16)
    lg_ref[0] = lax.dot_general(wrt_ref[...], h2, (((1,), (1,)), ((), ())),
                                precision=HIGHEST, preferred_element_type=F32)


def _outproj_call(attn, conv, x, mod_l, w_out_b, g2, w_router_t, tm=512):
    b, s, d = x.shape
    half = attn.shape[-1]
    ne = w_router_t.shape[0]
    return pl.pallas_call(
        _outproj_kernel,
        grid=(b, s // tm),
        in_specs=[
            pl.BlockSpec((1, tm, half), lambda i, j: (i, j, 0)),
            pl.BlockSpec((1, tm, half), lambda i, j: (i, j, 0)),
            pl.BlockSpec((1, tm, d), lambda i, j: (i, j, 0)),
            pl.BlockSpec((1, N_MOD, d), lambda i, j: (i, 0, 0)),
            pl.BlockSpec((2 * half, d), lambda i, j: (0, 0)),
            pl.BlockSpec((1, d), lambda i, j: (0, 0)),
            pl.BlockSpec((ne, d), lambda i, j: (0, 0)),
        ],
        out_specs=[
            pl.BlockSpec((1, tm, d), lambda i, j: (i, j, 0)),
            pl.BlockSpec((1, tm, d), lambda i, j: (i, j, 0)),
            pl.BlockSpec((1, ne, tm), lambda i, j: (i, 0, j)),
        ],
        out_shape=[
            jax.ShapeDtypeStruct((b, s, d), F32),
            jax.ShapeDtypeStruct((b, s, d), BF16),
            jax.ShapeDtypeStruct((b, ne, s), F32),
        ],
        compiler_params=_cparams(("parallel", "parallel"), 48),
        name="out_proj",
    )(attn, conv, x, mod_l, w_out_b, g2, w_router_t)


def _prefix_lanes(xb):
    r, s = xb.shape
    row = lax.broadcasted_iota(jnp.int32, (LANES, LANES), 0)
    col = lax.broadcasted_iota(jnp.int32, (LANES, LANES), 1)
    tri = jnp.where(row <= col, 1.0, 0.0).astype(BF16)
    carry = jnp.zeros((r, 1), F32)
    outs = []
    for blk in range(s // LANES):
        pb = jnp.dot(xb[:, blk * LANES:(blk + 1) * LANES], tri, preferred_element_type=F32) + carry
        outs.append(pb)
        carry = pb[:, LANES - 1:LANES]
    return jnp.concatenate(outs, axis=-1)


def _route_kernel(lg_ref, pos_ref, gate_ref, *, cap):
    lg = lg_ref[0]
    ne = lg.shape[0]
    m = jnp.max(lg, axis=0, keepdims=True)
    ex = jnp.exp(lg - m)
    aff = ex / jnp.sum(ex, axis=0, keepdims=True)
    bits = pltpu.bitcast(aff, jnp.int32)
    cur = jnp.zeros((ne, 1), jnp.int32)
    for bit in range(30, -1, -1):
        cand = cur | (1 << bit)
        cnt = jnp.sum(jnp.where(bits >= cand, 1.0, 0.0), axis=-1, keepdims=True)
        cur = jnp.where(cnt >= cap, cand, cur)
    gt = bits > cur
    eq = bits == cur
    gt_b = jnp.where(gt, 1.0, 0.0).astype(BF16)
    eq_b = jnp.where(eq, 1.0, 0.0).astype(BF16)
    cgt = _prefix_lanes(gt_b)
    ceq = _prefix_lanes(eq_b)
    n_gt = cgt[:, -1:]
    need = cap - n_gt
    sel = gt | (eq & (ceq <= need))
    rank = cgt + jnp.minimum(ceq, need) - 1.0
    pos_ref[0] = jnp.where(sel, rank, -1.0).astype(jnp.int32)
    gate_ref[0] = aff


def _route_call(logits_t, cap):
    b, ne, s = logits_t.shape
    return pl.pallas_call(
        functools.partial(_route_kernel, cap=cap),
        grid=(b,),
        in_specs=[pl.BlockSpec((1, ne, s), lambda i: (i, 0, 0))],
        out_specs=[pl.BlockSpec((1, ne, s), lambda i: (i, 0, 0))] * 2,
        out_shape=[jax.ShapeDtypeStruct((b, ne, s), jnp.int32), jax.ShapeDtypeStruct((b, ne, s), F32)],
        compiler_params=_cparams(("parallel",), 32),
        name="route",
    )(logits_t)


def _moe_kernel(pos_ref, gate_ref, h2_ref, x1_ref, mod_ref, wgu_ref, wd_ref, o_ref, *, cap, tb):
    e = pl.program_id(1)
    s = h2_ref.shape[1]
    ff = wd_ref.shape[1]

    @pl.when(e == 0)
    def _():
        o_ref[0] = x1_ref[0]

    pos = pos_ref[0, 0]
    slot = lax.broadcasted_iota(jnp.int32, (cap, s), 0)
    hit = pos == slot
    onehot = jnp.where(hit, 1.0, 0.0).astype(BF16)
    gate_c = jnp.sum(jnp.where(hit, gate_ref[0, 0], 0.0), axis=-1, keepdims=True)
    xe = jnp.dot(onehot, h2_ref[0], preferred_element_type=F32).astype(BF16)
    gu = jnp.dot(xe, wgu_ref[0], preferred_element_type=F32)
    g = gu[:, :ff]
    hid = (g * jax.nn.sigmoid(g)) * gu[:, ff:]
    ye = jnp.dot(hid.astype(BF16), wd_ref[0], preferred_element_type=F32) * gate_c
    ye_b = ye.astype(BF16)
    gate2 = mod_ref[0, 5:6, :]
    for blk in range(s // tb):
        rows = slice(blk * tb, (blk + 1) * tb)
        contrib = lax.dot_general(onehot[:, rows], ye_b, (((0,), (0,)), ((), ())), preferred_element_type=F32)
        o_ref[0, rows, :] = o_ref[0, rows, :] + gate2 * contrib


def _moe_call(pos, gate, h2, x1, mod_l, wgu_b, wd_b, cap, tb=256):
    b, s, d = x1.shape
    ne = wgu_b.shape[0]
    ff = wd_b.shape[1]
    pos4 = pos.reshape(b, ne, 1, s)
    gate4 = gate.reshape(b, ne, 1, s)
    return pl.pallas_call(
        functools.partial(_moe_kernel, cap=cap, tb=tb),
        grid=(b, ne),
        in_specs=[
            pl.BlockSpec((1, 1, 1, s), lambda i, e: (i, e, 0, 0)),
            pl.BlockSpec((1, 1, 1, s), lambda i, e: (i, e, 0, 0)),
            pl.BlockSpec((1, s, d), lambda i, e: (i, 0, 0)),
            pl.BlockSpec((1, s, d), lambda i, e: (i, 0, 0), pipeline_mode=pl.Buffered(1)),
            pl.BlockSpec((1, N_MOD, d), lambda i, e: (i, 0, 0)),
            pl.BlockSpec((1, d, 2 * ff), lambda i, e: (e, 0, 0)),
            pl.BlockSpec((1, ff, d), lambda i, e: (e, 0, 0)),
        ],
        out_specs=pl.BlockSpec((1, s, d), lambda i, e: (i, 0, 0)),
        out_shape=jax.ShapeDtypeStruct((b, s, d), F32),
        compiler_params=_cparams(("parallel", "arbitrary"), 56),
        name="moe_experts",
    )(pos4, gate4, h2, x1, mod_l, wgu_b, wd_b)


def _permute_weights(w_in, w_uq, w_ukv):
    o1 = Q_RANK
    o2 = o1 + KV_RANK
    o3 = o2 + ROPE
    w_in_p = jnp.concatenate([w_in[..., :o2], w_in[..., o3:], w_in[..., o2:o3]], axis=-1)
    depth, qr, _ = w_uq.shape
    wq = w_uq.reshape(depth, qr, N_HEADS, QK_DIM)
    w_uq_p = jnp.concatenate([wq[..., :NOPE].reshape(depth, qr, N_HEADS * NOPE),
                              wq[..., NOPE:].reshape(depth, qr, N_HEADS * ROPE)], axis=-1)
    kr = w_ukv.shape[1]
    wkv = w_ukv.reshape(depth, kr, N_HEADS, NOPE + V_DIM)
    w_ukv_p = jnp.concatenate([wkv[..., :NOPE].reshape(depth, kr, N_HEADS * NOPE),
                               wkv[..., NOPE:].reshape(depth, kr, N_HEADS * V_DIM)], axis=-1)
    return w_in_p.astype(BF16), w_uq_p.astype(BF16), w_ukv_p.astype(BF16)


def kernel(x, c, positions, norm1_g, w_ada, b_ada, w_in, q_latent_g, w_uq, kv_latent_g, w_ukv, q_head_g, k_head_g, conv_w, conv_b, conv_norm_g, conv_norm_b, w_out, norm2_g, w_router, w_gate, w_up, w_down):
    b, s, d = x.shape
    depth = w_ada.shape[0]
    cap = 2 * s // N_EXPERTS

    mod = _ada_call(c, w_ada, b_ada).reshape(depth, b, N_MOD, d)
    cos4, sin4 = _rope_tables(positions)
    w_in_p, w_uq_p, w_ukv_p = _permute_weights(w_in, w_uq, w_ukv)
    w_out_b = w_out.astype(BF16)
    wgu_b = jnp.concatenate([w_gate, w_up], axis=-1).astype(BF16)
    wd_b = w_down.astype(BF16)
    w_router_t = jnp.swapaxes(w_router, 1, 2)

    for l in range(depth):
        gqn = q_head_g[l, :NOPE].reshape(1, NOPE)
        gqr2 = jnp.tile(q_head_g[l, NOPE:], 2).reshape(1, LANES)
        gkn = k_head_g[l, :NOPE].reshape(1, NOPE)
        gkr2 = jnp.tile(k_head_g[l, NOPE:], 2).reshape(1, LANES)
        q, k, v, y = _inproj_call(x, mod[l], norm1_g[l].reshape(1, d), w_in_p[l],
                                  q_latent_g[l].reshape(1, Q_RANK), w_uq_p[l],
                                  kv_latent_g[l].reshape(1, KV_RANK), w_ukv_p[l],
                                  gqn, gqr2, gkn, gkr2, cos4, sin4)
        attn = _attn_call(q, k, v)
        conv = _conv_call(y, conv_w[l], conv_b[l].reshape(1, CONV_CH),
                          conv_norm_g[l].reshape(1, CONV_CH), conv_norm_b[l].reshape(1, CONV_CH))
        x1, h2, logits_t = _outproj_call(attn, conv, x, mod[l], w_out_b[l],
                                         norm2_g[l].reshape(1, d), w_router_t[l])
        pos, gate = _route_call(logits_t, cap)
        x = _moe_call(pos, gate, h2, x1, mod[l], wgu_b[l], wd_b[l], cap)
    return x
```

```python
import functools
import math

import jax
import jax.numpy as jnp
from jax import lax
from jax.experimental import pallas as pl
from jax.experimental.pallas import tpu as pltpu

F32 = jnp.float32
BF16 = jnp.bfloat16
HIGHEST = lax.Precision.HIGHEST

D_MODEL = 1024
N_HEADS = 4
NOPE = 128
ROPE = 64
QK_DIM = NOPE + ROPE
V_DIM = 128
Q_RANK = 256
KV_RANK = 128
CONV_CH = 512
CONV_WIDTH = 31
CONV_PAD = CONV_WIDTH // 2
N_EXPERTS = 16
EXPERT_FF = 1024
N_MOD = 6
ROPE_THETA = 10000.0
EPS = 1e-6

LANES = 128
SUBLANES = 8
HALO = 16
MIB = 1024 * 1024


def _cparams(sem, vmem_mib):
    return pltpu.CompilerParams(dimension_semantics=sem, vmem_limit_bytes=vmem_mib * MIB)


def _ada_kernel(c_ref, w_ref, b_ref, o_ref):
    c = c_ref[...]
    ca = c * jax.nn.sigmoid(c)
    o_ref[0] = jnp.dot(ca, w_ref[0], precision=HIGHEST, preferred_element_type=F32) + b_ref[0]


def _ada_call(c, w_ada, b_ada):
    depth, d, n = w_ada.shape
    b = c.shape[0]
    tn = 1536
    return pl.pallas_call(
        _ada_kernel,
        grid=(depth, n // tn),
        in_specs=[
            pl.BlockSpec((b, d), lambda l, j: (0, 0)),
            pl.BlockSpec((1, d, tn), lambda l, j: (l, 0, j)),
            pl.BlockSpec((1, 1, tn), lambda l, j: (l, 0, j)),
        ],
        out_specs=pl.BlockSpec((1, b, tn), lambda l, j: (l, 0, j)),
        out_shape=jax.ShapeDtypeStruct((depth, b, n), F32),
        compiler_params=_cparams(("parallel", "parallel"), 40),
        name="adaln_mod",
    )(c, w_ada, b_ada.reshape(depth, 1, n))


def _rope_kernel(pos_ref, invf_ref, cos_ref, sin_ref):
    ang = pos_ref[0].astype(F32) * invf_ref[...]
    cos_ref[0] = jnp.cos(ang)
    sin_ref[0] = jnp.sin(ang)


def _rope_tables(positions):
    b, s = positions.shape
    half = ROPE // 2
    per_row = LANES // half
    inv_freq = 1.0 / (ROPE_THETA ** (jnp.arange(0, ROPE, 2, dtype=F32) / ROPE))
    invf = jnp.tile(inv_freq, per_row).reshape(1, LANES)
    pos_rep = jnp.repeat(positions.reshape(b, s // per_row, per_row), half, axis=-1)
    cos_p, sin_p = pl.pallas_call(
        _rope_kernel,
        grid=(b,),
        in_specs=[
            pl.BlockSpec((1, s // per_row, LANES), lambda i: (i, 0, 0)),
            pl.BlockSpec((1, LANES), lambda i: (0, 0)),
        ],
        out_specs=[pl.BlockSpec((1, s // per_row, LANES), lambda i: (i, 0, 0))] * 2,
        out_shape=[jax.ShapeDtypeStruct((b, s // per_row, LANES), F32)] * 2,
        compiler_params=_cparams(("parallel",), 32),
        name="rope_tables",
    )(pos_rep, invf)
    cos = cos_p.reshape(b, s, half)
    sin = sin_p.reshape(b, s, half)
    cos4 = jnp.concatenate([cos, cos, cos, cos], axis=-1)
    sin4 = jnp.concatenate([-sin, sin, -sin, sin], axis=-1)
    return cos4, sin4


def _inproj_kernel(x_ref, mod_ref, g1_ref, win_ref, qlg_ref, wuq_ref, kvlg_ref, wukv_ref,
                   gqn_ref, gqr_ref, gkn_ref, gkr_ref, cos_ref, sin_ref,
                   q_ref, k_ref, v_ref, y_ref, *, sub):
    for sb in range(x_ref.shape[1] // sub):
        _inproj_rows(slice(sb * sub, (sb + 1) * sub), x_ref, mod_ref, g1_ref, win_ref, qlg_ref, wuq_ref, kvlg_ref,
                     wukv_ref, gqn_ref, gqr_ref, gkn_ref, gkr_ref, cos_ref, sin_ref, q_ref, k_ref, v_ref, y_ref)


def _inproj_rows(rows, x_ref, mod_ref, g1_ref, win_ref, qlg_ref, wuq_ref, kvlg_ref, wukv_ref,
                 gqn_ref, gqr_ref, gkn_ref, gkr_ref, cos_ref, sin_ref, q_ref, k_ref, v_ref, y_ref):
    x = x_ref[0, rows, :]
    tm = x.shape[0]
    shift1 = mod_ref[0, 0:1, :]
    scale1 = mod_ref[0, 1:2, :]
    h = x * lax.rsqrt(jnp.mean(x * x, axis=-1, keepdims=True) + EPS) * g1_ref[...]
    h = h * (1.0 + scale1) + shift1
    proj = jnp.dot(h.astype(BF16), win_ref[...], preferred_element_type=F32)
    o1 = Q_RANK
    o2 = o1 + KV_RANK
    o3 = o2 + CONV_CH
    o4 = o3 + CONV_CH
    c_q = proj[:, :o1]
    c_kv = proj[:, o1:o2]
    y_ref[0, rows, :] = proj[:, o2:o3] * jax.nn.sigmoid(proj[:, o3:o4])
    kr = proj[:, o4:]

    cqn = c_q * lax.rsqrt(jnp.mean(c_q * c_q, axis=-1, keepdims=True) + EPS) * qlg_ref[...]
    qf = jnp.dot(cqn.astype(BF16), wuq_ref[...], preferred_element_type=F32)
    ckn = c_kv * lax.rsqrt(jnp.mean(c_kv * c_kv, axis=-1, keepdims=True) + EPS) * kvlg_ref[...]
    kvf = jnp.dot(ckn.astype(BF16), wukv_ref[...], preferred_element_type=F32)

    cos4 = cos_ref[0, rows, :]
    sin4 = sin_ref[0, rows, :]
    lane = lax.broadcasted_iota(jnp.int32, (tm, LANES), 1)
    first_half = (lane & (ROPE // 2)) == 0

    def rope_pair(xb):
        swapped = jnp.where(first_half, pltpu.roll(xb, LANES - ROPE // 2, 1), pltpu.roll(xb, ROPE // 2, 1))
        return xb * cos4 + swapped * sin4

    kr_rot = rope_pair(jnp.concatenate([kr, kr], axis=-1) * gkr_ref[...])[:, :ROPE]
    kr_ss = jnp.sum(kr * kr, axis=-1, keepdims=True)
    nn = N_HEADS * NOPE
    for hh in range(N_HEADS):
        kn = kvf[:, hh * NOPE:(hh + 1) * NOPE]
        ss = jnp.sum(kn * kn, axis=-1, keepdims=True) + kr_ss
        r = lax.rsqrt(ss / QK_DIM + EPS)
        k_ref[0, hh, rows, :NOPE] = (kn * r * gkn_ref[...]).astype(BF16)
        k_ref[0, hh, rows, NOPE:] = (kr_rot * r).astype(BF16)
        v_ref[0, hh, rows, :] = kvf[:, nn + hh * V_DIM:nn + (hh + 1) * V_DIM].astype(BF16)

    qscale = math.log2(math.e) / math.sqrt(QK_DIM)
    for pair in range(N_HEADS // 2):
        qr = qf[:, nn + pair * LANES:nn + (pair + 1) * LANES]
        qr_rot = rope_pair(qr * gqr_ref[...])
        qr_sq = qr * qr
        for j in range(2):
            hh = 2 * pair + j
            qn = qf[:, hh * NOPE:(hh + 1) * NOPE]
            ss = (jnp.sum(qn * qn, axis=-1, keepdims=True)
                  + jnp.sum(qr_sq[:, j * ROPE:(j + 1) * ROPE], axis=-1, keepdims=True))
            r = lax.rsqrt(ss / QK_DIM + EPS) * qscale
            q_ref[0, hh, rows, :NOPE] = (qn * r * gqn_ref[...]).astype(BF16)
            q_ref[0, hh, rows, NOPE:] = (qr_rot[:, j * ROPE:(j + 1) * ROPE] * r).astype(BF16)


def _inproj_call(x, mod_l, g1, w_in_p, qlg, w_uq_p, kvlg, w_ukv_p, gqn, gqr2, gkn, gkr2, cos4, sin4, tm=1024, sub=256):
    b, s, d = x.shape
    full = lambda shape: pl.BlockSpec(shape, lambda i, j: (0,) * len(shape))
    return pl.pallas_call(
        functools.partial(_inproj_kernel, sub=sub),
        grid=(b, s // tm),
        in_specs=[
            pl.BlockSpec((1, tm, d), lambda i, j: (i, j, 0)),
            pl.BlockSpec((1, N_MOD, d), lambda i, j: (i, 0, 0)),
            full((1, d)),
            full(w_in_p.shape),
            full((1, Q_RANK)),
            full(w_uq_p.shape),
            full((1, KV_RANK)),
            full(w_ukv_p.shape),
            full((1, NOPE)),
            full((1, LANES)),
            full((1, NOPE)),
            full((1, LANES)),
            pl.BlockSpec((1, tm, LANES), lambda i, j: (i, j, 0)),
            pl.BlockSpec((1, tm, LANES), lambda i, j: (i, j, 0)),
        ],
        out_specs=[
            pl.BlockSpec((1, N_HEADS, tm, QK_DIM), lambda i, j: (i, 0, j, 0)),
            pl.BlockSpec((1, N_HEADS, tm, QK_DIM), lambda i, j: (i, 0, j, 0)),
            pl.BlockSpec((1, N_HEADS, tm, V_DIM), lambda i, j: (i, 0, j, 0)),
            pl.BlockSpec((1, tm, CONV_CH), lambda i, j: (i, j, 0)),
        ],
        out_shape=[
            jax.ShapeDtypeStruct((b, N_HEADS, s, QK_DIM), BF16),
            jax.ShapeDtypeStruct((b, N_HEADS, s, QK_DIM), BF16),
            jax.ShapeDtypeStruct((b, N_HEADS, s, V_DIM), BF16),
            jax.ShapeDtypeStruct((b, s, CONV_CH), F32),
        ],
        compiler_params=_cparams(("parallel", "parallel"), 48),
        name="in_proj",
    )(x, mod_l, g1, w_in_p, qlg, w_uq_p, kvlg, w_ukv_p, gqn, gqr2, gkn, gkr2, cos4, sin4)


def _attn_kernel(q_ref, k_ref, v_ref, o_ref, *, sub):
    dv = v_ref.shape[-1]
    for hh in range(q_ref.shape[1]):
        k = k_ref[0, hh]
        v = v_ref[0, hh]
        for sb in range(q_ref.shape[2] // sub):
            rows = slice(sb * sub, (sb + 1) * sub)
            s2 = lax.dot_general(q_ref[0, hh, rows, :], k, (((1,), (1,)), ((), ())), preferred_element_type=F32)
            m = jnp.max(s2, axis=-1, keepdims=True)
            p = jnp.exp2(s2 - m)
            l = jnp.sum(p, axis=-1, keepdims=True)
            o = jnp.dot(p.astype(BF16), v, preferred_element_type=F32)
            o_ref[0, rows, hh * dv:(hh + 1) * dv] = (o / l).astype(BF16)


def _attn_call(q, k, v, hp=2, sub=512):
    b, nh, s, dk = q.shape
    dv = v.shape[-1]
    return pl.pallas_call(
        functools.partial(_attn_kernel, sub=sub),
        grid=(b, nh // hp),
        in_specs=[
            pl.BlockSpec((1, hp, s, dk), lambda i, h: (i, h, 0, 0)),
            pl.BlockSpec((1, hp, s, dk), lambda i, h: (i, h, 0, 0)),
            pl.BlockSpec((1, hp, s, dv), lambda i, h: (i, h, 0, 0)),
        ],
        out_specs=pl.BlockSpec((1, s, hp * dv), lambda i, h: (i, 0, h)),
        out_shape=jax.ShapeDtypeStruct((b, s, nh * dv), BF16),
        compiler_params=_cparams(("parallel", "parallel"), 48),
        name="attention",
    )(q, k, v)


def _conv_kernel(yc_ref, yp_ref, yn_ref, w_ref, cb_ref, lng_ref, lnb_ref, o_ref, win_ref, *, rc, rs):
    i = pl.program_id(1)
    nblk = pl.num_programs(1)
    win_ref[0, 0:HALO, :] = jnp.where(i > 0, yp_ref[0], 0.0)
    win_ref[0, HALO:HALO + rc, :] = yc_ref[0]
    win_ref[0, HALO + rc:HALO + rc + HALO, :] = jnp.where(i < nblk - 1, yn_ref[0], 0.0)
    span = rc + HALO + SUBLANES
    for sh in range(1, SUBLANES):
        win_ref[sh, 0:span, :] = win_ref[0, sh:sh + span, :]
    nch = CONV_CH // LANES
    for sub in range(rc // rs):
        r0 = sub * rs
        accs = [jnp.zeros((rs, LANES), F32) + cb_ref[:, c * LANES:(c + 1) * LANES] for c in range(nch)]
        for kk in range(CONV_WIDTH):
            off = HALO - CONV_PAD + kk
            base = r0 + off - off % SUBLANES
            for c in range(nch):
                cs = slice(c * LANES, (c + 1) * LANES)
                accs[c] = accs[c] + win_ref[off % SUBLANES, base:base + rs, cs] * w_ref[kk:kk + 1, cs]
        tot = accs[0]
        for c in range(1, nch):
            tot = tot + accs[c]
        mu = jnp.sum(tot, axis=-1, keepdims=True) / CONV_CH
        sq = jnp.zeros((rs, LANES), F32)
        for c in range(nch):
            dlt = accs[c] - mu
            sq = sq + dlt * dlt
        var = jnp.sum(sq, axis=-1, keepdims=True) / CONV_CH
        rstd = lax.rsqrt(var + EPS)
        for c in range(nch):
            cs = slice(c * LANES, (c + 1) * LANES)
            yn = (accs[c] - mu) * rstd * lng_ref[:, cs] + lnb_ref[:, cs]
            o_ref[0, r0:r0 + rs, cs] = (yn * jax.nn.sigmoid(yn)).astype(BF16)


def _conv_call(y, conv_w, conv_b, ln_g, ln_b, rc=256, rs=32):
    b, s, ch = y.shape
    hb = rc // HALO
    nh = s // HALO
    return pl.pallas_call(
        functools.partial(_conv_kernel, rc=rc, rs=rs),
        grid=(b, s // rc),
        in_specs=[
            pl.BlockSpec((1, rc, ch), lambda i, j: (i, j, 0)),
            pl.BlockSpec((1, HALO, ch), lambda i, j: (i, jnp.maximum(j * hb - 1, 0), 0)),
            pl.BlockSpec((1, HALO, ch), lambda i, j: (i, jnp.minimum((j + 1) * hb, nh - 1), 0)),
            pl.BlockSpec((CONV_WIDTH, ch), lambda i, j: (0, 0)),
            pl.BlockSpec((1, ch), lambda i, j: (0, 0)),
            pl.BlockSpec((1, ch), lambda i, j: (0, 0)),
            pl.BlockSpec((1, ch), lambda i, j: (0, 0)),
        ],
        out_specs=pl.BlockSpec((1, rc, ch), lambda i, j: (i, j, 0)),
        out_shape=jax.ShapeDtypeStruct((b, s, ch), BF16),
        scratch_shapes=[pltpu.VMEM((SUBLANES, rc + 2 * HALO, ch), F32)],
        compiler_params=_cparams(("parallel", "parallel"), 32),
        name="conv_group",
    )(y, y, y, conv_w, conv_b, ln_g, ln_b)


def _store_token_tiles(ref, lead, row0, val):
    r, d = val.shape
    nc = d // LANES
    for g in range(r // SUBLANES):
        for c in range(nc):
            dst = pl.ds((row0 + g * SUBLANES) * nc + c, SUBLANES, stride=nc)
            ref[lead + (dst, slice(None))] = val[g * SUBLANES:(g + 1) * SUBLANES, c * LANES:(c + 1) * LANES]


def _load_token_tiles(ref, lead, row0, r, d):
    nc = d // LANES
    groups = []
    for g in range(r // SUBLANES):
        groups.append(jnp.concatenate(
            [ref[lead + (pl.ds((row0 + g * SUBLANES) * nc + c, SUBLANES, stride=nc), slice(None))]
             for c in range(nc)], axis=-1))
    return jnp.concatenate(groups, axis=0)


def _outproj_kernel(a_ref, cv_ref, x_ref, mod_ref, wo_ref, g2_ref, wrt_ref, x1_ref, h2_ref, lg_ref, *, sub):
    half = a_ref.shape[-1]
    gate1 = mod_ref[0, 2:3, :]
    shift2 = mod_ref[0, 3:4, :]
    scale2 = mod_ref[0, 4:5, :]
    for sb in range(x_ref.shape[1] // sub):
        rows = slice(sb * sub, (sb + 1) * sub)
        mix = (jnp.dot(a_ref[0, rows, :], wo_ref[:half, :], preferred_element_type=F32)
               + jnp.dot(cv_ref[0, rows, :], wo_ref[half:, :], preferred_element_type=F32))
        x1 = x_ref[0, rows, :] + gate1 * mix
        x1_ref[0, rows, :] = x1
        h2 = x1 * lax.rsqrt(jnp.mean(x1 * x1, axis=-1, keepdims=True) + EPS) * g2_ref[...]
        h2 = h2 * (1.0 + scale2) + shift2
        _store_token_tiles(h2_ref, (0,), sb * sub, h2)
        lg_ref[0, :, rows] = lax.dot_general(wrt_ref[...], h2, (((1,), (1,)), ((), ())),
                                             precision=HIGHEST, preferred_element_type=F32)


def _outproj_call(attn, conv, x, mod_l, w_out_b, g2, w_router_t, tm=1024, sub=256):
    b, s, d = x.shape
    half = attn.shape[-1]
    ne = w_router_t.shape[0]
    return pl.pallas_call(
        functools.partial(_outproj_kernel, sub=sub),
        grid=(b, s // tm),
        in_specs=[
            pl.BlockSpec((1, tm, half), lambda i, j: (i, j, 0)),
            pl.BlockSpec((1, tm, half), lambda i, j: (i, j, 0)),
            pl.BlockSpec((1, tm, d), lambda i, j: (i, j, 0)),
            pl.BlockSpec((1, N_MOD, d), lambda i, j: (i, 0, 0)),
            pl.BlockSpec((2 * half, d), lambda i, j: (0, 0)),
            pl.BlockSpec((1, d), lambda i, j: (0, 0)),
            pl.BlockSpec((ne, d), lambda i, j: (0, 0)),
        ],
        out_specs=[
            pl.BlockSpec((1, tm, d), lambda i, j: (i, j, 0)),
            pl.BlockSpec((1, tm * d // LANES, LANES), lambda i, j: (i, j, 0)),
            pl.BlockSpec((1, ne, tm), lambda i, j: (i, 0, j)),
        ],
        out_shape=[
            jax.ShapeDtypeStruct((b, s, d), F32),
            jax.ShapeDtypeStruct((b, s * d // LANES, LANES), F32),
            jax.ShapeDtypeStruct((b, ne, s), F32),
        ],
        compiler_params=_cparams(("parallel", "parallel"), 48),
        name="out_proj",
    )(attn, conv, x, mod_l, w_out_b, g2, w_router_t)


def _prefix_lanes(xb):
    r, s = xb.shape
    row = lax.broadcasted_iota(jnp.int32, (LANES, LANES), 0)
    col = lax.broadcasted_iota(jnp.int32, (LANES, LANES), 1)
    tri = jnp.where(row <= col, 1.0, 0.0).astype(BF16)
    carry = jnp.zeros((r, 1), F32)
    outs = []
    for blk in range(s // LANES):
        pb = jnp.dot(xb[:, blk * LANES:(blk + 1) * LANES], tri, preferred_element_type=F32) + carry
        outs.append(pb)
        carry = pb[:, LANES - 1:LANES]
    return jnp.concatenate(outs, axis=-1)


def _route_kernel(lg_ref, idx_ref, gate_ref, *, cap, refine):
    lg = lg_ref[0]
    ne, s = lg.shape
    idx_bits = (s - 1).bit_length()
    m = jnp.max(lg, axis=0, keepdims=True)
    ex = jnp.exp(lg - m)
    aff = ex / jnp.sum(ex, axis=0, keepdims=True)

    def count_ge(thr):
        return jnp.sum(jnp.where(aff >= thr, 1.0, 0.0), axis=-1, keepdims=True)

    cur = jnp.zeros((ne, 1), jnp.int32)
    for bit in range(30, -1, -1):
        cand = cur | (1 << bit)
        cur = jnp.where(count_ge(pltpu.bitcast(cand, F32)) >= cap, cand, cur)
    lo = pltpu.bitcast(cur, F32)
    hi = pltpu.bitcast(cur + 1, F32)
    for _ in range(refine):
        mid = 0.5 * (lo + hi)
        ge = count_ge(mid) >= cap
        lo = jnp.where(ge, mid, lo)
        hi = jnp.where(ge, hi, mid)
    above = aff >= hi
    window = (aff >= lo) & jnp.logical_not(above)
    c_above = _prefix_lanes(jnp.where(above, 1.0, 0.0).astype(BF16))
    c_win = _prefix_lanes(jnp.where(window, 1.0, 0.0).astype(BF16))
    need = cap - c_above[:, -1:]
    sel = above | (window & (c_win <= need))
    rank = (c_above + jnp.minimum(c_win, need)).astype(jnp.int32) - 1
    t = lax.broadcasted_iota(jnp.int32, (ne, s), 1)
    w = jnp.where(sel, ((t - rank) << (idx_bits + 1)) | (t << 1) | 1, 0)
    g = aff
    for kk in range(idx_bits):
        step = 1 << kk
        w_in = pltpu.roll(w, s - step, 1)
        g_in = pltpu.roll(g, s - step, 1)
        take = ((w_in >> (idx_bits + 1 + kk)) & w_in & 1) == 1
        stay = ((w >> (idx_bits + 1 + kk)) & 1) == 0
        w = jnp.where(take, w_in, jnp.where(stay, w, 0))
        g = jnp.where(take, g_in, g)
    idx_ref[0] = (w[:, :cap] >> 1) & ((1 << idx_bits) - 1)
    gate_ref[0] = g[:, :cap]


def _route_call(logits_t, cap, refine=6):
    b, ne, s = logits_t.shape
    return pl.pallas_call(
        functools.partial(_route_kernel, cap=cap, refine=refine),
        grid=(b,),
        in_specs=[pl.BlockSpec((1, ne, s), lambda i: (i, 0, 0))],
        out_specs=[pl.BlockSpec((1, ne, cap), lambda i: (i, 0, 0))] * 2,
        out_shape=[jax.ShapeDtypeStruct((b, ne, cap), jnp.int32), jax.ShapeDtypeStruct((b, ne, cap), F32)],
        compiler_params=_cparams(("parallel",), 32),
        name="route",
    )(logits_t)


def _moe_kernel(idx_ref, gate_ref, h2_ref, wgu_ref, wd_ref, acc_ref, xe_ref, ye_ref, *, cap, grp, d):
    e = pl.program_id(1)
    ff = wd_ref.shape[1]
    nc = d // LANES

    @pl.when(e == 0)
    def _():
        acc_ref[...] = jnp.zeros(acc_ref.shape, F32)

    def token_rows(tok):
        return pl.ds(pl.multiple_of(tok * nc, nc), nc)

    def gather(i, carry):
        for u in range(grp):
            j = i * grp + u
            xe_ref[token_rows(j), :] = h2_ref[0, token_rows(idx_ref[0, 0, 0, j]), :]
        return carry

    lax.fori_loop(0, cap // grp, gather, 0)
    xe = _load_token_tiles(xe_ref, (), 0, cap, d).astype(BF16)
    gu = jnp.dot(xe, wgu_ref[0], preferred_element_type=F32)
    g = gu[:, :ff]
    hid = (g * jax.nn.sigmoid(g)) * gu[:, ff:]
    _store_token_tiles(ye_ref, (), 0, jnp.dot(hid.astype(BF16), wd_ref[0], preferred_element_type=F32))

    def scatter(i, carry):
        toks = [idx_ref[0, 0, 0, i * grp + u] for u in range(grp)]
        new = [acc_ref[0, token_rows(toks[u]), :]
               + gate_ref[0, 0, 0, i * grp + u] * ye_ref[token_rows(i * grp + u), :] for u in range(grp)]
        for u in range(grp):
            acc_ref[0, token_rows(toks[u]), :] = new[u]
        return carry

    lax.fori_loop(0, cap // grp, scatter, 0)


def _moe_call(idx, gate, h2_tt, wgu_b, wd_b, d, grp=8):
    b, rows, _ = h2_tt.shape
    ne, cap = idx.shape[1:]
    ff = wd_b.shape[1]
    nc = d // LANES
    return pl.pallas_call(
        functools.partial(_moe_kernel, cap=cap, grp=grp, d=d),
        grid=(b, ne),
        in_specs=[
            pl.BlockSpec((1, 1, 1, cap), lambda i, e: (i, e, 0, 0), memory_space=pltpu.SMEM),
            pl.BlockSpec((1, 1, 1, cap), lambda i, e: (i, e, 0, 0), memory_space=pltpu.SMEM),
            pl.BlockSpec((1, rows, LANES), lambda i, e: (i, 0, 0)),
            pl.BlockSpec((1, d, 2 * ff), lambda i, e: (e, 0, 0)),
            pl.BlockSpec((1, ff, d), lambda i, e: (e, 0, 0)),
        ],
        out_specs=pl.BlockSpec((1, rows, LANES), lambda i, e: (i, 0, 0)),
        out_shape=jax.ShapeDtypeStruct((b, rows, LANES), F32),
        scratch_shapes=[pltpu.VMEM((cap * nc, LANES), F32), pltpu.VMEM((cap * nc, LANES), F32)],
        compiler_params=_cparams(("parallel", "arbitrary"), 56),
        name="moe_experts",
    )(idx.reshape(b, ne, 1, cap), gate.reshape(b, ne, 1, cap), h2_tt, wgu_b, wd_b)


def _combine_kernel(x1_ref, acc_ref, mod_ref, o_ref, *, sub):
    gate2 = mod_ref[0, 5:6, :]
    d = x1_ref.shape[-1]
    for sb in range(x1_ref.shape[1] // sub):
        rows = slice(sb * sub, (sb + 1) * sub)
        o_ref[0, rows, :] = x1_ref[0, rows, :] + gate2 * _load_token_tiles(acc_ref, (0,), sb * sub, sub, d)


def _combine_call(x1, acc_tt, mod_l, tm=512, sub=128):
    b, s, d = x1.shape
    return pl.pallas_call(
        functools.partial(_combine_kernel, sub=sub),
        grid=(b, s // tm),
        in_specs=[
            pl.BlockSpec((1, tm, d), lambda i, j: (i, j, 0)),
            pl.BlockSpec((1, tm * d // LANES, LANES), lambda i, j: (i, j, 0)),
            pl.BlockSpec((1, N_MOD, d), lambda i, j: (i, 0, 0)),
        ],
        out_specs=pl.BlockSpec((1, tm, d), lambda i, j: (i, j, 0)),
        out_shape=jax.ShapeDtypeStruct((b, s, d), F32),
        compiler_params=_cparams(("parallel", "parallel"), 32),
        name="moe_combine",
    )(x1, acc_tt, mod_l)


def _permute_weights(w_in, w_uq, w_ukv):
    o1 = Q_RANK
    o2 = o1 + KV_RANK
    o3 = o2 + ROPE
    w_in_p = jnp.concatenate([w_in[..., :o2], w_in[..., o3:], w_in[..., o2:o3]], axis=-1)
    depth, qr, _ = w_uq.shape
    wq = w_uq.reshape(depth, qr, N_HEADS, QK_DIM)
    w_uq_p = jnp.concatenate([wq[..., :NOPE].reshape(depth, qr, N_HEADS * NOPE),
                              wq[..., NOPE:].reshape(depth, qr, N_HEADS * ROPE)], axis=-1)
    kr = w_ukv.shape[1]
    wkv = w_ukv.reshape(depth, kr, N_HEADS, NOPE + V_DIM)
    w_ukv_p = jnp.concatenate([wkv[..., :NOPE].reshape(depth, kr, N_HEADS * NOPE),
                               wkv[..., NOPE:].reshape(depth, kr, N_HEADS * V_DIM)], axis=-1)
    return w_in_p.astype(BF16), w_uq_p.astype(BF16), w_ukv_p.astype(BF16)


def kernel(x, c, positions, norm1_g, w_ada, b_ada, w_in, q_latent_g, w_uq, kv_latent_g, w_ukv, q_head_g, k_head_g, conv_w, conv_b, conv_norm_g, conv_norm_b, w_out, norm2_g, w_router, w_gate, w_up, w_down):
    b, s, d = x.shape
    depth = w_ada.shape[0]
    cap = 2 * s // N_EXPERTS

    mod = _ada_call(c, w_ada, b_ada).reshape(depth, b, N_MOD, d)
    cos4, sin4 = _rope_tables(positions)
    w_in_p, w_uq_p, w_ukv_p = _permute_weights(w_in, w_uq, w_ukv)
    w_out_b = w_out.astype(BF16)
    wgu_b = jnp.concatenate([w_gate, w_up], axis=-1).astype(BF16)
    wd_b = w_down.astype(BF16)
    w_router_t = jnp.swapaxes(w_router, 1, 2)

    for l in range(depth):
        gqn = q_head_g[l, :NOPE].reshape(1, NOPE)
        gqr2 = jnp.tile(q_head_g[l, NOPE:], 2).reshape(1, LANES)
        gkn = k_head_g[l, :NOPE].reshape(1, NOPE)
        gkr2 = jnp.tile(k_head_g[l, NOPE:], 2).reshape(1, LANES)
        q, k, v, y = _inproj_call(x, mod[l], norm1_g[l].reshape(1, d), w_in_p[l],
                                  q_latent_g[l].reshape(1, Q_RANK), w_uq_p[l],
                                  kv_latent_g[l].reshape(1, KV_RANK), w_ukv_p[l],
                                  gqn, gqr2, gkn, gkr2, cos4, sin4)
        attn = _attn_call(q, k, v)
        conv = _conv_call(y, conv_w[l], conv_b[l].reshape(1, CONV_CH),
                          conv_norm_g[l].reshape(1, CONV_CH), conv_norm_b[l].reshape(1, CONV_CH))
        x1, h2_tt, logits_t = _outproj_call(attn, conv, x, mod[l], w_out_b[l],
                                            norm2_g[l].reshape(1, d), w_router_t[l])
        idx, gate = _route_call(logits_t, cap)
        acc_tt = _moe_call(idx, gate, h2_tt, wgu_b[l], wd_b[l], d)
        x = _combine_call(x1, acc_tt, mod[l])
    return x
```

```python
import functools
import math

import jax
import jax.numpy as jnp
from jax import lax
from jax.experimental import pallas as pl
from jax.experimental.pallas import tpu as pltpu

F32 = jnp.float32
BF16 = jnp.bfloat16
HIGHEST = lax.Precision.HIGHEST

D_MODEL = 1024
N_HEADS = 4
NOPE = 128
ROPE = 64
QK_DIM = NOPE + ROPE
V_DIM = 128
Q_RANK = 256
KV_RANK = 128
CONV_CH = 512
CONV_WIDTH = 31
CONV_PAD = CONV_WIDTH // 2
N_EXPERTS = 16
EXPERT_FF = 1024
N_MOD = 6
ROPE_THETA = 10000.0
EPS = 1e-6

LANES = 128
SUBLANES = 8
HALO = 16
MIB = 1024 * 1024


def _cparams(sem, vmem_mib):
    return pltpu.CompilerParams(dimension_semantics=sem, vmem_limit_bytes=vmem_mib * MIB)


def _ada_kernel(c_ref, w_ref, b_ref, o_ref):
    c = c_ref[...]
    ca = c * jax.nn.sigmoid(c)
    o_ref[0] = jnp.dot(ca, w_ref[0], precision=HIGHEST, preferred_element_type=F32) + b_ref[0]


def _ada_call(c, w_ada, b_ada):
    depth, d, n = w_ada.shape
    b = c.shape[0]
    tn = 1536
    return pl.pallas_call(
        _ada_kernel,
        grid=(depth, n // tn),
        in_specs=[
            pl.BlockSpec((b, d), lambda l, j: (0, 0)),
            pl.BlockSpec((1, d, tn), lambda l, j: (l, 0, j)),
            pl.BlockSpec((1, 1, tn), lambda l, j: (l, 0, j)),
        ],
        out_specs=pl.BlockSpec((1, b, tn), lambda l, j: (l, 0, j)),
        out_shape=jax.ShapeDtypeStruct((depth, b, n), F32),
        compiler_params=_cparams(("parallel", "parallel"), 40),
        name="adaln_mod",
    )(c, w_ada, b_ada.reshape(depth, 1, n))


def _rope_kernel(pos_ref, invf_ref, cos_ref, sin_ref):
    half = ROPE // 2
    per_row = LANES // half
    ang = pos_ref[0].astype(F32) * invf_ref[...]
    packed_cos = jnp.cos(ang)
    packed_sin = jnp.sin(ang)
    n = ang.shape[0]
    group = lax.broadcasted_iota(jnp.int32, ang.shape, 1) // half
    sign = jnp.where(group % 2 == 0, -1.0, 1.0)

    def spread(p, j):
        out = p
        for q in range(per_row):
            if q != j:
                out = jnp.where(group == q, pltpu.roll(p, (half * (q - j)) % LANES, 1), out)
        return out

    for j in range(per_row):
        cos_ref[0, pl.ds(j, n, stride=per_row), :] = spread(packed_cos, j)
        sin_ref[0, pl.ds(j, n, stride=per_row), :] = spread(packed_sin, j) * sign


def _rope_tables(positions):
    b, s = positions.shape
    half = ROPE // 2
    per_row = LANES // half
    inv_freq = 1.0 / (ROPE_THETA ** (jnp.arange(0, ROPE, 2, dtype=F32) / ROPE))
    invf = jnp.tile(inv_freq, per_row).reshape(1, LANES)
    pos_rep = jnp.repeat(positions.reshape(b, s // per_row, per_row), half, axis=-1)
    return pl.pallas_call(
        _rope_kernel,
        grid=(b,),
        in_specs=[
            pl.BlockSpec((1, s // per_row, LANES), lambda i: (i, 0, 0)),
            pl.BlockSpec((1, LANES), lambda i: (0, 0)),
        ],
        out_specs=[pl.BlockSpec((1, s, LANES), lambda i: (i, 0, 0))] * 2,
        out_shape=[jax.ShapeDtypeStruct((b, s, LANES), F32)] * 2,
        compiler_params=_cparams(("parallel",), 32),
        name="rope_tables",
    )(pos_rep, invf)


def _inproj_kernel(x_ref, mod_ref, g1_ref, win_ref, qlg_ref, wuq_ref, kvlg_ref, wukv_ref,
                   gqn_ref, gqr_ref, gkn_ref, gkr_ref, cos_ref, sin_ref,
                   q_ref, k_ref, v_ref, y_ref, *, sub):
    for sb in range(x_ref.shape[1] // sub):
        _inproj_rows(slice(sb * sub, (sb + 1) * sub), x_ref, mod_ref, g1_ref, win_ref, qlg_ref, wuq_ref, kvlg_ref,
                     wukv_ref, gqn_ref, gqr_ref, gkn_ref, gkr_ref, cos_ref, sin_ref, q_ref, k_ref, v_ref, y_ref)


def _inproj_rows(rows, x_ref, mod_ref, g1_ref, win_ref, qlg_ref, wuq_ref, kvlg_ref, wukv_ref,
                 gqn_ref, gqr_ref, gkn_ref, gkr_ref, cos_ref, sin_ref, q_ref, k_ref, v_ref, y_ref):
    x = x_ref[0, rows, :]
    tm = x.shape[0]
    shift1 = mod_ref[0, 0:1, :]
    scale1 = mod_ref[0, 1:2, :]
    h = x * lax.rsqrt(jnp.mean(x * x, axis=-1, keepdims=True) + EPS) * g1_ref[...]
    h = h * (1.0 + scale1) + shift1
    proj = jnp.dot(h.astype(BF16), win_ref[...], preferred_element_type=F32)
    o1 = Q_RANK
    o2 = o1 + KV_RANK
    o3 = o2 + CONV_CH
    o4 = o3 + CONV_CH
    c_q = proj[:, :o1]
    c_kv = proj[:, o1:o2]
    y_ref[0, rows, :] = proj[:, o2:o3] * jax.nn.sigmoid(proj[:, o3:o4])
    kr = proj[:, o4:]

    cqn = c_q * lax.rsqrt(jnp.mean(c_q * c_q, axis=-1, keepdims=True) + EPS) * qlg_ref[...]
    qf = jnp.dot(cqn.astype(BF16), wuq_ref[...], preferred_element_type=F32)
    ckn = c_kv * lax.rsqrt(jnp.mean(c_kv * c_kv, axis=-1, keepdims=True) + EPS) * kvlg_ref[...]
    kvf = jnp.dot(ckn.astype(BF16), wukv_ref[...], preferred_element_type=F32)

    cos4 = cos_ref[0, rows, :]
    sin4 = sin_ref[0, rows, :]
    lane = lax.broadcasted_iota(jnp.int32, (tm, LANES), 1)
    first_half = (lane & (ROPE // 2)) == 0

    def rope_pair(xb):
        swapped = jnp.where(first_half, pltpu.roll(xb, LANES - ROPE // 2, 1), pltpu.roll(xb, ROPE // 2, 1))
        return xb * cos4 + swapped * sin4

    kr_rot = rope_pair(jnp.concatenate([kr, kr], axis=-1) * gkr_ref[...])[:, :ROPE]
    kr_ss = jnp.sum(kr * kr, axis=-1, keepdims=True)
    nn = N_HEADS * NOPE
    for hh in range(N_HEADS):
        kn = kvf[:, hh * NOPE:(hh + 1) * NOPE]
        ss = jnp.sum(kn * kn, axis=-1, keepdims=True) + kr_ss
        r = lax.rsqrt(ss / QK_DIM + EPS)
        k_ref[0, hh, rows, :NOPE] = (kn * r * gkn_ref[...]).astype(BF16)
        k_ref[0, hh, rows, NOPE:] = (kr_rot * r).astype(BF16)
        v_ref[0, hh, rows, :] = kvf[:, nn + hh * V_DIM:nn + (hh + 1) * V_DIM].astype(BF16)

    qscale = math.log2(math.e) / math.sqrt(QK_DIM)
    for pair in range(N_HEADS // 2):
        qr = qf[:, nn + pair * LANES:nn + (pair + 1) * LANES]
        qr_rot = rope_pair(qr * gqr_ref[...])
        qr_sq = qr * qr
        for j in range(2):
            hh = 2 * pair + j
            qn = qf[:, hh * NOPE:(hh + 1) * NOPE]
            ss = (jnp.sum(qn * qn, axis=-1, keepdims=True)
                  + jnp.sum(qr_sq[:, j * ROPE:(j + 1) * ROPE], axis=-1, keepdims=True))
            r = lax.rsqrt(ss / QK_DIM + EPS) * qscale
            q_ref[0, hh, rows, :NOPE] = (qn * r * gqn_ref[...]).astype(BF16)
            q_ref[0, hh, rows, NOPE:] = (qr_rot[:, j * ROPE:(j + 1) * ROPE] * r).astype(BF16)


def _inproj_call(x, mod_l, g1, w_in_p, qlg, w_uq_p, kvlg, w_ukv_p, gqn, gqr2, gkn, gkr2, cos4, sin4, tm=1024, sub=256):
    b, s, d = x.shape
    full = lambda shape: pl.BlockSpec(shape, lambda i, j: (0,) * len(shape))
    return pl.pallas_call(
        functools.partial(_inproj_kernel, sub=sub),
        grid=(b, s // tm),
        in_specs=[
            pl.BlockSpec((1, tm, d), lambda i, j: (i, j, 0)),
            pl.BlockSpec((1, N_MOD, d), lambda i, j: (i, 0, 0)),
            full((1, d)),
            full(w_in_p.shape),
            full((1, Q_RANK)),
            full(w_uq_p.shape),
            full((1, KV_RANK)),
            full(w_ukv_p.shape),
            full((1, NOPE)),
            full((1, LANES)),
            full((1, NOPE)),
            full((1, LANES)),
            pl.BlockSpec((1, tm, LANES), lambda i, j: (i, j, 0)),
            pl.BlockSpec((1, tm, LANES), lambda i, j: (i, j, 0)),
        ],
        out_specs=[
            pl.BlockSpec((1, N_HEADS, tm, QK_DIM), lambda i, j: (i, 0, j, 0)),
            pl.BlockSpec((1, N_HEADS, tm, QK_DIM), lambda i, j: (i, 0, j, 0)),
            pl.BlockSpec((1, N_HEADS, tm, V_DIM), lambda i, j: (i, 0, j, 0)),
            pl.BlockSpec((1, tm, CONV_CH), lambda i, j: (i, j, 0)),
        ],
        out_shape=[
            jax.ShapeDtypeStruct((b, N_HEADS, s, QK_DIM), BF16),
            jax.ShapeDtypeStruct((b, N_HEADS, s, QK_DIM), BF16),
            jax.ShapeDtypeStruct((b, N_HEADS, s, V_DIM), BF16),
            jax.ShapeDtypeStruct((b, s, CONV_CH), F32),
        ],
        compiler_params=_cparams(("parallel", "parallel"), 48),
        name="in_proj",
    )(x, mod_l, g1, w_in_p, qlg, w_uq_p, kvlg, w_ukv_p, gqn, gqr2, gkn, gkr2, cos4, sin4)


def _attn_kernel(q_ref, k_ref, v_ref, o_ref, *, sub):
    dv = v_ref.shape[-1]
    for hh in range(q_ref.shape[1]):
        k = k_ref[0, hh]
        v = v_ref[0, hh]
        for sb in range(q_ref.shape[2] // sub):
            rows = slice(sb * sub, (sb + 1) * sub)
            s2 = lax.dot_general(q_ref[0, hh, rows, :], k, (((1,), (1,)), ((), ())), preferred_element_type=F32)
            m = jnp.max(s2, axis=-1, keepdims=True)
            p = jnp.exp2(s2 - m)
            l = jnp.sum(p, axis=-1, keepdims=True)
            o = jnp.dot(p.astype(BF16), v, preferred_element_type=F32)
            o_ref[0, rows, hh * dv:(hh + 1) * dv] = (o / l).astype(BF16)


def _attn_call(q, k, v, hp=2, sub=512):
    b, nh, s, dk = q.shape
    dv = v.shape[-1]
    return pl.pallas_call(
        functools.partial(_attn_kernel, sub=sub),
        grid=(b, nh // hp),
        in_specs=[
            pl.BlockSpec((1, hp, s, dk), lambda i, h: (i, h, 0, 0)),
            pl.BlockSpec((1, hp, s, dk), lambda i, h: (i, h, 0, 0)),
            pl.BlockSpec((1, hp, s, dv), lambda i, h: (i, h, 0, 0)),
        ],
        out_specs=pl.BlockSpec((1, s, hp * dv), lambda i, h: (i, 0, h)),
        out_shape=jax.ShapeDtypeStruct((b, s, nh * dv), BF16),
        compiler_params=_cparams(("parallel", "parallel"), 48),
        name="attention",
    )(q, k, v)


def _conv_kernel(yc_ref, yp_ref, yn_ref, w_ref, cb_ref, lng_ref, lnb_ref, o_ref, win_ref, *, rc, rs):
    i = pl.program_id(1)
    nblk = pl.num_programs(1)
    win_ref[0, 0:HALO, :] = jnp.where(i > 0, yp_ref[0], 0.0)
    win_ref[0, HALO:HALO + rc, :] = yc_ref[0]
    win_ref[0, HALO + rc:HALO + rc + HALO, :] = jnp.where(i < nblk - 1, yn_ref[0], 0.0)
    span = rc + HALO + SUBLANES
    for sh in range(1, SUBLANES):
        win_ref[sh, 0:span, :] = win_ref[0, sh:sh + span, :]
    nch = CONV_CH // LANES
    for sub in range(rc // rs):
        r0 = sub * rs
        accs = [jnp.zeros((rs, LANES), F32) + cb_ref[:, c * LANES:(c + 1) * LANES] for c in range(nch)]
        for kk in range(CONV_WIDTH):
            off = HALO - CONV_PAD + kk
            base = r0 + off - off % SUBLANES
            for c in range(nch):
                cs = slice(c * LANES, (c + 1) * LANES)
                accs[c] = accs[c] + win_ref[off % SUBLANES, base:base + rs, cs] * w_ref[kk:kk + 1, cs]
        tot = accs[0]
        for c in range(1, nch):
            tot = tot + accs[c]
        mu = jnp.sum(tot, axis=-1, keepdims=True) / CONV_CH
        sq = jnp.zeros((rs, LANES), F32)
        for c in range(nch):
            dlt = accs[c] - mu
            sq = sq + dlt * dlt
        var = jnp.sum(sq, axis=-1, keepdims=True) / CONV_CH
        rstd = lax.rsqrt(var + EPS)
        for c in range(nch):
            cs = slice(c * LANES, (c + 1) * LANES)
            yn = (accs[c] - mu) * rstd * lng_ref[:, cs] + lnb_ref[:, cs]
            o_ref[0, r0:r0 + rs, cs] = (yn * jax.nn.sigmoid(yn)).astype(BF16)


def _conv_call(y, conv_w, conv_b, ln_g, ln_b, rc=256, rs=32):
    b, s, ch = y.shape
    hb = rc // HALO
    nh = s // HALO
    return pl.pallas_call(
        functools.partial(_conv_kernel, rc=rc, rs=rs),
        grid=(b, s // rc),
        in_specs=[
            pl.BlockSpec((1, rc, ch), lambda i, j: (i, j, 0)),
            pl.BlockSpec((1, HALO, ch), lambda i, j: (i, jnp.maximum(j * hb - 1, 0), 0)),
            pl.BlockSpec((1, HALO, ch), lambda i, j: (i, jnp.minimum((j + 1) * hb, nh - 1), 0)),
            pl.BlockSpec((CONV_WIDTH, ch), lambda i, j: (0, 0)),
            pl.BlockSpec((1, ch), lambda i, j: (0, 0)),
            pl.BlockSpec((1, ch), lambda i, j: (0, 0)),
            pl.BlockSpec((1, ch), lambda i, j: (0, 0)),
        ],
        out_specs=pl.BlockSpec((1, rc, ch), lambda i, j: (i, j, 0)),
        out_shape=jax.ShapeDtypeStruct((b, s, ch), BF16),
        scratch_shapes=[pltpu.VMEM((SUBLANES, rc + 2 * HALO, ch), F32)],
        compiler_params=_cparams(("parallel", "parallel"), 32),
        name="conv_group",
    )(y, y, y, conv_w, conv_b, ln_g, ln_b)


def _store_token_tiles(ref, lead, row0, val):
    r, d = val.shape
    nc = d // LANES
    for g in range(r // SUBLANES):
        for c in range(nc):
            dst = pl.ds((row0 + g * SUBLANES) * nc + c, SUBLANES, stride=nc)
            ref[lead + (dst, slice(None))] = val[g * SUBLANES:(g + 1) * SUBLANES, c * LANES:(c + 1) * LANES]


def _load_token_tiles(ref, lead, row0, r, d):
    nc = d // LANES
    groups = []
    for g in range(r // SUBLANES):
        groups.append(jnp.concatenate(
            [ref[lead + (pl.ds((row0 + g * SUBLANES) * nc + c, SUBLANES, stride=nc), slice(None))]
             for c in range(nc)], axis=-1))
    return jnp.concatenate(groups, axis=0)


def _outproj_kernel(a_ref, cv_ref, x_ref, mod_ref, wo_ref, g2_ref, wrt_ref, x1_ref, h2_ref, lg_ref, *, sub):
    half = a_ref.shape[-1]
    gate1 = mod_ref[0, 2:3, :]
    shift2 = mod_ref[0, 3:4, :]
    scale2 = mod_ref[0, 4:5, :]
    for sb in range(x_ref.shape[1] // sub):
        rows = slice(sb * sub, (sb + 1) * sub)
        mix = (jnp.dot(a_ref[0, rows, :], wo_ref[:half, :], preferred_element_type=F32)
               + jnp.dot(cv_ref[0, rows, :], wo_ref[half:, :], preferred_element_type=F32))
        x1 = x_ref[0, rows, :] + gate1 * mix
        x1_ref[0, rows, :] = x1
        h2 = x1 * lax.rsqrt(jnp.mean(x1 * x1, axis=-1, keepdims=True) + EPS) * g2_ref[...]
        h2 = h2 * (1.0 + scale2) + shift2
        _store_token_tiles(h2_ref, (0,), sb * sub, h2)
        lg_ref[0, :, rows] = lax.dot_general(wrt_ref[...], h2, (((1,), (1,)), ((), ())),
                                             precision=HIGHEST, preferred_element_type=F32)


def _outproj_call(attn, conv, x, mod_l, w_out_b, g2, w_router_t, tm=1024, sub=256):
    b, s, d = x.shape
    half = attn.shape[-1]
    ne = w_router_t.shape[0]
    return pl.pallas_call(
        functools.partial(_outproj_kernel, sub=sub),
        grid=(b, s // tm),
        in_specs=[
            pl.BlockSpec((1, tm, half), lambda i, j: (i, j, 0)),
            pl.BlockSpec((1, tm, half), lambda i, j: (i, j, 0)),
            pl.BlockSpec((1, tm, d), lambda i, j: (i, j, 0)),
            pl.BlockSpec((1, N_MOD, d), lambda i, j: (i, 0, 0)),
            pl.BlockSpec((2 * half, d), lambda i, j: (0, 0)),
            pl.BlockSpec((1, d), lambda i, j: (0, 0)),
            pl.BlockSpec((ne, d), lambda i, j: (0, 0)),
        ],
        out_specs=[
            pl.BlockSpec((1, tm, d), lambda i, j: (i, j, 0)),
            pl.BlockSpec((1, tm * d // LANES, LANES), lambda i, j: (i, j, 0)),
            pl.BlockSpec((1, ne, tm), lambda i, j: (i, 0, j)),
        ],
        out_shape=[
            jax.ShapeDtypeStruct((b, s, d), F32),
            jax.ShapeDtypeStruct((b, s * d // LANES, LANES), F32),
            jax.ShapeDtypeStruct((b, ne, s), F32),
        ],
        compiler_params=_cparams(("parallel", "parallel"), 48),
        name="out_proj",
    )(attn, conv, x, mod_l, w_out_b, g2, w_router_t)


def _prefix_lanes(xb):
    r, s = xb.shape
    row = lax.broadcasted_iota(jnp.int32, (LANES, LANES), 0)
    col = lax.broadcasted_iota(jnp.int32, (LANES, LANES), 1)
    tri = jnp.where(row <= col, 1.0, 0.0).astype(BF16)
    carry = jnp.zeros((r, 1), F32)
    outs = []
    for blk in range(s // LANES):
        pb = jnp.dot(xb[:, blk * LANES:(blk + 1) * LANES], tri, preferred_element_type=F32) + carry
        outs.append(pb)
        carry = pb[:, LANES - 1:LANES]
    return jnp.concatenate(outs, axis=-1)


def _route_kernel(lg_ref, idx_ref, gate_ref, *, cap, refine):
    lg = lg_ref[0]
    ne, s = lg.shape
    idx_bits = (s - 1).bit_length()
    m = jnp.max(lg, axis=0, keepdims=True)
    ex = jnp.exp(lg - m)
    aff = ex / jnp.sum(ex, axis=0, keepdims=True)

    def count_ge(thr):
        return jnp.sum(jnp.where(aff >= thr, 1.0, 0.0), axis=-1, keepdims=True)

    cur = jnp.zeros((ne, 1), jnp.int32)
    for bit in range(30, -1, -1):
        cand = cur | (1 << bit)
        cur = jnp.where(count_ge(pltpu.bitcast(cand, F32)) >= cap, cand, cur)
    lo = pltpu.bitcast(cur, F32)
    hi = pltpu.bitcast(cur + 1, F32)
    for _ in range(refine):
        mid = 0.5 * (lo + hi)
        ge = count_ge(mid) >= cap
        lo = jnp.where(ge, mid, lo)
        hi = jnp.where(ge, hi, mid)
    above = aff >= hi
    window = (aff >= lo) & jnp.logical_not(above)
    c_above = _prefix_lanes(jnp.where(above, 1.0, 0.0).astype(BF16))
    c_win = _prefix_lanes(jnp.where(window, 1.0, 0.0).astype(BF16))
    need = cap - c_above[:, -1:]
    sel = above | (window & (c_win <= need))
    rank = (c_above + jnp.minimum(c_win, need)).astype(jnp.int32) - 1
    t = lax.broadcasted_iota(jnp.int32, (ne, s), 1)
    w = jnp.where(sel, ((t - rank) << (idx_bits + 1)) | (t << 1) | 1, 0)
    g = aff
    for kk in range(idx_bits):
        step = 1 << kk
        w_in = pltpu.roll(w, s - step, 1)
        g_in = pltpu.roll(g, s - step, 1)
        take = ((w_in >> (idx_bits + 1 + kk)) & w_in & 1) == 1
        stay = ((w >> (idx_bits + 1 + kk)) & 1) == 0
        w = jnp.where(take, w_in, jnp.where(stay, w, 0))
        g = jnp.where(take, g_in, g)
    idx_ref[0] = (w[:, :cap] >> 1) & ((1 << idx_bits) - 1)
    gate_ref[0] = g[:, :cap]


def _route_call(logits_t, cap, refine=6):
    b, ne, s = logits_t.shape
    return pl.pallas_call(
        functools.partial(_route_kernel, cap=cap, refine=refine),
        grid=(b,),
        in_specs=[pl.BlockSpec((1, ne, s), lambda i: (i, 0, 0))],
        out_specs=[pl.BlockSpec((1, ne, cap), lambda i: (i, 0, 0))] * 2,
        out_shape=[jax.ShapeDtypeStruct((b, ne, cap), jnp.int32), jax.ShapeDtypeStruct((b, ne, cap), F32)],
        compiler_params=_cparams(("parallel",), 32),
        name="route",
    )(logits_t)


def _moe_kernel(idx_ref, gate_ref, idxp_ref, gatep_ref, idxn_ref, h2_ref, wg_ref, wu_ref, wd_ref, acc_ref,
                xe_ref, ye_ref, *, cap, grp, d):
    e = pl.program_id(1)
    last = pl.num_programs(1) - 1
    nc = d // LANES
    slot = e % 2
    other = 1 - slot

    def token_rows(tok):
        return pl.ds(pl.multiple_of(tok * nc, nc), nc)

    def accumulate(i_ref, g_ref, y_slot, i):
        toks = [i_ref[0, 0, 0, i * grp + u] for u in range(grp)]
        new = [acc_ref[0, token_rows(toks[u]), :]
               + g_ref[0, 0, 0, i * grp + u] * ye_ref[y_slot, token_rows(i * grp + u), :] for u in range(grp)]
        for u in range(grp):
            acc_ref[0, token_rows(toks[u]), :] = new[u]

    @pl.when(e == 0)
    def _():
        acc_ref[...] = jnp.zeros(acc_ref.shape, F32)
        ye_ref[1] = jnp.zeros(ye_ref.shape[1:], F32)

        def gather(i, carry):
            for u in range(grp):
                j = i * grp + u
                xe_ref[0, token_rows(j), :] = h2_ref[0, token_rows(idx_ref[0, 0, 0, j]), :]
            return carry

        lax.fori_loop(0, cap // grp, gather, 0)

    xe = _load_token_tiles(xe_ref, (slot,), 0, cap, d).astype(BF16)
    for i in range(cap // grp):
        accumulate(idxp_ref, gatep_ref, other, i)
    for j in range(cap):
        xe_ref[other, token_rows(j), :] = h2_ref[0, token_rows(idxn_ref[0, 0, 0, j]), :]
    g = jnp.dot(xe, wg_ref[0], preferred_element_type=F32)
    hid = (g * jax.nn.sigmoid(g)) * jnp.dot(xe, wu_ref[0], preferred_element_type=F32)
    _store_token_tiles(ye_ref, (slot,), 0, jnp.dot(hid.astype(BF16), wd_ref[0], preferred_element_type=F32))

    @pl.when(e == last)
    def _():
        def tail(i, carry):
            accumulate(idx_ref, gate_ref, slot, i)
            return carry

        lax.fori_loop(0, cap // grp, tail, 0)


def _moe_call(idx, gate, h2_tt, wg_b, wu_b, wd_b, d, grp=8):
    b, rows, _ = h2_tt.shape
    ne, cap = idx.shape[1:]
    ff = wd_b.shape[1]
    nc = d // LANES
    idx4 = idx.reshape(b, ne, 1, cap)
    gate4 = gate.reshape(b, ne, 1, cap)

    def per_expert(index_map):
        return pl.BlockSpec((1, 1, 1, cap), index_map, memory_space=pltpu.SMEM)

    this_e = lambda i, e: (i, e, 0, 0)
    prev_e = lambda i, e: (i, jnp.maximum(e - 1, 0), 0, 0)
    next_e = lambda i, e: (i, jnp.minimum(e + 1, ne - 1), 0, 0)
    return pl.pallas_call(
        functools.partial(_moe_kernel, cap=cap, grp=grp, d=d),
        grid=(b, ne),
        in_specs=[
            per_expert(this_e), per_expert(this_e), per_expert(prev_e), per_expert(prev_e), per_expert(next_e),
            pl.BlockSpec((1, rows, LANES), lambda i, e: (i, 0, 0)),
            pl.BlockSpec((1, d, ff), lambda i, e: (e, 0, 0)),
            pl.BlockSpec((1, d, ff), lambda i, e: (e, 0, 0)),
            pl.BlockSpec((1, ff, d), lambda i, e: (e, 0, 0)),
        ],
        out_specs=pl.BlockSpec((1, rows, LANES), lambda i, e: (i, 0, 0)),
        out_shape=jax.ShapeDtypeStruct((b, rows, LANES), F32),
        scratch_shapes=[pltpu.VMEM((2, cap * nc, LANES), F32), pltpu.VMEM((2, cap * nc, LANES), F32)],
        compiler_params=_cparams(("arbitrary", "arbitrary"), 56),
        name="moe_experts",
    )(idx4, gate4, idx4, gate4, idx4, h2_tt, wg_b, wu_b, wd_b)


def _combine_kernel(x1_ref, acc_ref, mod_ref, o_ref, *, sub):
    gate2 = mod_ref[0, 5:6, :]
    d = x1_ref.shape[-1]
    for sb in range(x1_ref.shape[1] // sub):
        rows = slice(sb * sub, (sb + 1) * sub)
        o_ref[0, rows, :] = x1_ref[0, rows, :] + gate2 * _load_token_tiles(acc_ref, (0,), sb * sub, sub, d)


def _combine_call(x1, acc_tt, mod_l, tm=512, sub=128):
    b, s, d = x1.shape
    return pl.pallas_call(
        functools.partial(_combine_kernel, sub=sub),
        grid=(b, s // tm),
        in_specs=[
            pl.BlockSpec((1, tm, d), lambda i, j: (i, j, 0)),
            pl.BlockSpec((1, tm * d // LANES, LANES), lambda i, j: (i, j, 0)),
            pl.BlockSpec((1, N_MOD, d), lambda i, j: (i, 0, 0)),
        ],
        out_specs=pl.BlockSpec((1, tm, d), lambda i, j: (i, j, 0)),
        out_shape=jax.ShapeDtypeStruct((b, s, d), F32),
        compiler_params=_cparams(("parallel", "parallel"), 32),
        name="moe_combine",
    )(x1, acc_tt, mod_l)


def _permute_weights(w_in, w_uq, w_ukv):
    o1 = Q_RANK
    o2 = o1 + KV_RANK
    o3 = o2 + ROPE
    w_in_p = jnp.concatenate([w_in[..., :o2], w_in[..., o3:], w_in[..., o2:o3]], axis=-1)
    depth, qr, _ = w_uq.shape
    wq = w_uq.reshape(depth, qr, N_HEADS, QK_DIM)
    w_uq_p = jnp.concatenate([wq[..., :NOPE].reshape(depth, qr, N_HEADS * NOPE),
                              wq[..., NOPE:].reshape(depth, qr, N_HEADS * ROPE)], axis=-1)
    kr = w_ukv.shape[1]
    wkv = w_ukv.reshape(depth, kr, N_HEADS, NOPE + V_DIM)
    w_ukv_p = jnp.concatenate([wkv[..., :NOPE].reshape(depth, kr, N_HEADS * NOPE),
                               wkv[..., NOPE:].reshape(depth, kr, N_HEADS * V_DIM)], axis=-1)
    return w_in_p.astype(BF16), w_uq_p.astype(BF16), w_ukv_p.astype(BF16)


def kernel(x, c, positions, norm1_g, w_ada, b_ada, w_in, q_latent_g, w_uq, kv_latent_g, w_ukv, q_head_g, k_head_g, conv_w, conv_b, conv_norm_g, conv_norm_b, w_out, norm2_g, w_router, w_gate, w_up, w_down):
    b, s, d = x.shape
    depth = w_ada.shape[0]
    cap = 2 * s // N_EXPERTS

    mod = _ada_call(c, w_ada, b_ada).reshape(depth, b, N_MOD, d)
    cos4, sin4 = _rope_tables(positions)
    w_in_p, w_uq_p, w_ukv_p = _permute_weights(w_in, w_uq, w_ukv)
    w_out_b = w_out.astype(BF16)
    wg_b = w_gate.astype(BF16)
    wu_b = w_up.astype(BF16)
    wd_b = w_down.astype(BF16)
    w_router_t = jnp.swapaxes(w_router, 1, 2)

    for l in range(depth):
        gqn = q_head_g[l, :NOPE].reshape(1, NOPE)
        gqr2 = jnp.tile(q_head_g[l, NOPE:], 2).reshape(1, LANES)
        gkn = k_head_g[l, :NOPE].reshape(1, NOPE)
        gkr2 = jnp.tile(k_head_g[l, NOPE:], 2).reshape(1, LANES)
        q, k, v, y = _inproj_call(x, mod[l], norm1_g[l].reshape(1, d), w_in_p[l],
                                  q_latent_g[l].reshape(1, Q_RANK), w_uq_p[l],
                                  kv_latent_g[l].reshape(1, KV_RANK), w_ukv_p[l],
                                  gqn, gqr2, gkn, gkr2, cos4, sin4)
        attn = _attn_call(q, k, v)
        conv = _conv_call(y, conv_w[l], conv_b[l].reshape(1, CONV_CH),
                          conv_norm_g[l].reshape(1, CONV_CH), conv_norm_b[l].reshape(1, CONV_CH))
        x1, h2_tt, logits_t = _outproj_call(attn, conv, x, mod[l], w_out_b[l],
                                            norm2_g[l].reshape(1, d), w_router_t[l])
        idx, gate = _route_call(logits_t, cap)
        acc_tt = _moe_call(idx, gate, h2_tt, wg_b[l], wu_b[l], wd_b[l], d)
        x = _combine_call(x1, acc_tt, mod[l])
    return x
```

```python
import functools
import math

import jax
import jax.numpy as jnp
from jax import lax
from jax.experimental import pallas as pl
from jax.experimental.pallas import tpu as pltpu

F32 = jnp.float32
BF16 = jnp.bfloat16
HIGHEST = lax.Precision.HIGHEST

D_MODEL = 1024
N_HEADS = 4
NOPE = 128
ROPE = 64
QK_DIM = NOPE + ROPE
V_DIM = 128
Q_RANK = 256
KV_RANK = 128
CONV_CH = 512
CONV_WIDTH = 31
CONV_PAD = CONV_WIDTH // 2
N_EXPERTS = 16
EXPERT_FF = 1024
N_MOD = 6
ROPE_THETA = 10000.0
EPS = 1e-6

LANES = 128
SUBLANES = 8
HALO = 16
MIB = 1024 * 1024


def _cparams(sem, vmem_mib):
    return pltpu.CompilerParams(dimension_semantics=sem, vmem_limit_bytes=vmem_mib * MIB)


def _ada_kernel(c_ref, w_ref, b_ref, o_ref):
    c = c_ref[...]
    ca = c * jax.nn.sigmoid(c)
    o_ref[0] = jnp.dot(ca, w_ref[0], precision=HIGHEST, preferred_element_type=F32) + b_ref[0]


def _ada_call(c, w_ada, b_ada):
    depth, d, n = w_ada.shape
    b = c.shape[0]
    tn = 1536
    return pl.pallas_call(
        _ada_kernel,
        grid=(depth, n // tn),
        in_specs=[
            pl.BlockSpec((b, d), lambda l, j: (0, 0)),
            pl.BlockSpec((1, d, tn), lambda l, j: (l, 0, j)),
            pl.BlockSpec((1, 1, tn), lambda l, j: (l, 0, j)),
        ],
        out_specs=pl.BlockSpec((1, b, tn), lambda l, j: (l, 0, j)),
        out_shape=jax.ShapeDtypeStruct((depth, b, n), F32),
        compiler_params=_cparams(("parallel", "parallel"), 40),
        name="adaln_mod",
    )(c, w_ada, b_ada.reshape(depth, 1, n))


def _rope_kernel(pos_ref, invf_ref, cos_ref, sin_ref):
    half = ROPE // 2
    per_row = LANES // half
    ang = pos_ref[0].astype(F32) * invf_ref[...]
    packed_cos = jnp.cos(ang)
    packed_sin = jnp.sin(ang)
    n = ang.shape[0]
    group = lax.broadcasted_iota(jnp.int32, ang.shape, 1) // half
    sign = jnp.where(group % 2 == 0, -1.0, 1.0)

    def spread(p, j):
        out = p
        for q in range(per_row):
            if q != j:
                out = jnp.where(group == q, pltpu.roll(p, (half * (q - j)) % LANES, 1), out)
        return out

    for j in range(per_row):
        cos_ref[0, pl.ds(j, n, stride=per_row), :] = spread(packed_cos, j)
        sin_ref[0, pl.ds(j, n, stride=per_row), :] = spread(packed_sin, j) * sign


def _rope_tables(positions):
    b, s = positions.shape
    half = ROPE // 2
    per_row = LANES // half
    inv_freq = 1.0 / (ROPE_THETA ** (jnp.arange(0, ROPE, 2, dtype=F32) / ROPE))
    invf = jnp.tile(inv_freq, per_row).reshape(1, LANES)
    pos_rep = jnp.repeat(positions.reshape(b, s // per_row, per_row), half, axis=-1)
    return pl.pallas_call(
        _rope_kernel,
        grid=(b,),
        in_specs=[
            pl.BlockSpec((1, s // per_row, LANES), lambda i: (i, 0, 0)),
            pl.BlockSpec((1, LANES), lambda i: (0, 0)),
        ],
        out_specs=[pl.BlockSpec((1, s, LANES), lambda i: (i, 0, 0))] * 2,
        out_shape=[jax.ShapeDtypeStruct((b, s, LANES), F32)] * 2,
        compiler_params=_cparams(("parallel",), 32),
        name="rope_tables",
    )(pos_rep, invf)


def _inproj_kernel(x_ref, mod_ref, g1_ref, win_ref, qlg_ref, wuq_ref, kvlg_ref, wukv_ref,
                   gqn_ref, gqr_ref, gkn_ref, gkr_ref, cos_ref, sin_ref,
                   q_ref, k_ref, v_ref, y_ref, *, sub):
    for sb in range(x_ref.shape[1] // sub):
        _inproj_rows(slice(sb * sub, (sb + 1) * sub), x_ref, mod_ref, g1_ref, win_ref, qlg_ref, wuq_ref, kvlg_ref,
                     wukv_ref, gqn_ref, gqr_ref, gkn_ref, gkr_ref, cos_ref, sin_ref, q_ref, k_ref, v_ref, y_ref)


def _inproj_rows(rows, x_ref, mod_ref, g1_ref, win_ref, qlg_ref, wuq_ref, kvlg_ref, wukv_ref,
                 gqn_ref, gqr_ref, gkn_ref, gkr_ref, cos_ref, sin_ref, q_ref, k_ref, v_ref, y_ref):
    x = x_ref[0, rows, :]
    tm = x.shape[0]
    shift1 = mod_ref[0, 0:1, :]
    scale1 = mod_ref[0, 1:2, :]
    h = x * lax.rsqrt(jnp.mean(x * x, axis=-1, keepdims=True) + EPS) * g1_ref[...]
    h = h * (1.0 + scale1) + shift1
    proj = jnp.dot(h.astype(BF16), win_ref[...], preferred_element_type=F32)
    o1 = Q_RANK
    o2 = o1 + KV_RANK
    o3 = o2 + CONV_CH
    o4 = o3 + CONV_CH
    c_q = proj[:, :o1]
    c_kv = proj[:, o1:o2]
    y_ref[0, rows, :] = proj[:, o2:o3] * jax.nn.sigmoid(proj[:, o3:o4])
    kr = proj[:, o4:]

    cqn = c_q * lax.rsqrt(jnp.mean(c_q * c_q, axis=-1, keepdims=True) + EPS) * qlg_ref[...]
    qf = jnp.dot(cqn.astype(BF16), wuq_ref[...], preferred_element_type=F32)
    ckn = c_kv * lax.rsqrt(jnp.mean(c_kv * c_kv, axis=-1, keepdims=True) + EPS) * kvlg_ref[...]
    kvf = jnp.dot(ckn.astype(BF16), wukv_ref[...], preferred_element_type=F32)

    cos4 = cos_ref[0, rows, :]
    sin4 = sin_ref[0, rows, :]
    lane = lax.broadcasted_iota(jnp.int32, (tm, LANES), 1)
    first_half = (lane & (ROPE // 2)) == 0

    def rope_pair(xb):
        swapped = jnp.where(first_half, pltpu.roll(xb, LANES - ROPE // 2, 1), pltpu.roll(xb, ROPE // 2, 1))
        return xb * cos4 + swapped * sin4

    kr_rot = rope_pair(jnp.concatenate([kr, kr], axis=-1) * gkr_ref[...])[:, :ROPE]
    kr_ss = jnp.sum(kr * kr, axis=-1, keepdims=True)
    nn = N_HEADS * NOPE
    for hh in range(N_HEADS):
        kn = kvf[:, hh * NOPE:(hh + 1) * NOPE]
        ss = jnp.sum(kn * kn, axis=-1, keepdims=True) + kr_ss
        r = lax.rsqrt(ss / QK_DIM + EPS)
        k_ref[0, hh, rows, :NOPE] = (kn * r * gkn_ref[...]).astype(BF16)
        k_ref[0, hh, rows, NOPE:] = (kr_rot * r).astype(BF16)
        v_ref[0, hh, rows, :] = kvf[:, nn + hh * V_DIM:nn + (hh + 1) * V_DIM].astype(BF16)

    qscale = math.log2(math.e) / math.sqrt(QK_DIM)
    for pair in range(N_HEADS // 2):
        qr = qf[:, nn + pair * LANES:nn + (pair + 1) * LANES]
        qr_rot = rope_pair(qr * gqr_ref[...])
        qr_sq = qr * qr
        for j in range(2):
            hh = 2 * pair + j
            qn = qf[:, hh * NOPE:(hh + 1) * NOPE]
            ss = (jnp.sum(qn * qn, axis=-1, keepdims=True)
                  + jnp.sum(qr_sq[:, j * ROPE:(j + 1) * ROPE], axis=-1, keepdims=True))
            r = lax.rsqrt(ss / QK_DIM + EPS) * qscale
            q_ref[0, hh, rows, :NOPE] = (qn * r * gqn_ref[...]).astype(BF16)
            q_ref[0, hh, rows, NOPE:] = (qr_rot[:, j * ROPE:(j + 1) * ROPE] * r).astype(BF16)


def _inproj_call(x, mod_l, g1, w_in_p, qlg, w_uq_p, kvlg, w_ukv_p, gqn, gqr2, gkn, gkr2, cos4, sin4, tm=1024, sub=256):
    b, s, d = x.shape
    full = lambda shape: pl.BlockSpec(shape, lambda i, j: (0,) * len(shape))
    return pl.pallas_call(
        functools.partial(_inproj_kernel, sub=sub),
        grid=(b, s // tm),
        in_specs=[
            pl.BlockSpec((1, tm, d), lambda i, j: (i, j, 0)),
            pl.BlockSpec((1, N_MOD, d), lambda i, j: (i, 0, 0)),
            full((1, d)),
            full(w_in_p.shape),
            full((1, Q_RANK)),
            full(w_uq_p.shape),
            full((1, KV_RANK)),
            full(w_ukv_p.shape),
            full((1, NOPE)),
            full((1, LANES)),
            full((1, NOPE)),
            full((1, LANES)),
            pl.BlockSpec((1, tm, LANES), lambda i, j: (i, j, 0)),
            pl.BlockSpec((1, tm, LANES), lambda i, j: (i, j, 0)),
        ],
        out_specs=[
            pl.BlockSpec((1, N_HEADS, tm, QK_DIM), lambda i, j: (i, 0, j, 0)),
            pl.BlockSpec((1, N_HEADS, tm, QK_DIM), lambda i, j: (i, 0, j, 0)),
            pl.BlockSpec((1, N_HEADS, tm, V_DIM), lambda i, j: (i, 0, j, 0)),
            pl.BlockSpec((1, tm, CONV_CH), lambda i, j: (i, j, 0)),
        ],
        out_shape=[
            jax.ShapeDtypeStruct((b, N_HEADS, s, QK_DIM), BF16),
            jax.ShapeDtypeStruct((b, N_HEADS, s, QK_DIM), BF16),
            jax.ShapeDtypeStruct((b, N_HEADS, s, V_DIM), BF16),
            jax.ShapeDtypeStruct((b, s, CONV_CH), F32),
        ],
        compiler_params=_cparams(("parallel", "parallel"), 48),
        name="in_proj",
    )(x, mod_l, g1, w_in_p, qlg, w_uq_p, kvlg, w_ukv_p, gqn, gqr2, gkn, gkr2, cos4, sin4)


def _attn_kernel(q_ref, k_ref, v_ref, o_ref, *, sub):
    dv = v_ref.shape[-1]
    for hh in range(q_ref.shape[1]):
        k = k_ref[0, hh]
        v = v_ref[0, hh]
        for sb in range(q_ref.shape[2] // sub):
            rows = slice(sb * sub, (sb + 1) * sub)
            s2 = lax.dot_general(q_ref[0, hh, rows, :], k, (((1,), (1,)), ((), ())), preferred_element_type=F32)
            m = jnp.max(s2, axis=-1, keepdims=True)
            p = jnp.exp2(s2 - m)
            l = jnp.sum(p, axis=-1, keepdims=True)
            o = jnp.dot(p.astype(BF16), v, preferred_element_type=F32)
            o_ref[0, rows, hh * dv:(hh + 1) * dv] = (o / l).astype(BF16)


def _attn_call(q, k, v, hp=2, sub=512):
    b, nh, s, dk = q.shape
    dv = v.shape[-1]
    return pl.pallas_call(
        functools.partial(_attn_kernel, sub=sub),
        grid=(b, nh // hp),
        in_specs=[
            pl.BlockSpec((1, hp, s, dk), lambda i, h: (i, h, 0, 0)),
            pl.BlockSpec((1, hp, s, dk), lambda i, h: (i, h, 0, 0)),
            pl.BlockSpec((1, hp, s, dv), lambda i, h: (i, h, 0, 0)),
        ],
        out_specs=pl.BlockSpec((1, s, hp * dv), lambda i, h: (i, 0, h)),
        out_shape=jax.ShapeDtypeStruct((b, s, nh * dv), BF16),
        compiler_params=_cparams(("parallel", "parallel"), 48),
        name="attention",
    )(q, k, v)


def _conv_kernel(yc_ref, yp_ref, yn_ref, w_ref, cb_ref, lng_ref, lnb_ref, o_ref, win_ref, *, rc, rs):
    i = pl.program_id(1)
    nblk = pl.num_programs(1)
    win_ref[0, 0:HALO, :] = jnp.where(i > 0, yp_ref[0], 0.0)
    win_ref[0, HALO:HALO + rc, :] = yc_ref[0]
    win_ref[0, HALO + rc:HALO + rc + HALO, :] = jnp.where(i < nblk - 1, yn_ref[0], 0.0)
    span = rc + HALO + SUBLANES
    for sh in range(1, SUBLANES):
        win_ref[sh, 0:span, :] = win_ref[0, sh:sh + span, :]
    nch = CONV_CH // LANES
    for sub in range(rc // rs):
        r0 = sub * rs
        accs = [jnp.zeros((rs, LANES), F32) + cb_ref[:, c * LANES:(c + 1) * LANES] for c in range(nch)]
        for kk in range(CONV_WIDTH):
            off = HALO - CONV_PAD + kk
            base = r0 + off - off % SUBLANES
            for c in range(nch):
                cs = slice(c * LANES, (c + 1) * LANES)
                accs[c] = accs[c] + win_ref[off % SUBLANES, base:base + rs, cs] * w_ref[kk:kk + 1, cs]
        tot = accs[0]
        for c in range(1, nch):
            tot = tot + accs[c]
        mu = jnp.sum(tot, axis=-1, keepdims=True) / CONV_CH
        sq = jnp.zeros((rs, LANES), F32)
        for c in range(nch):
            dlt = accs[c] - mu
            sq = sq + dlt * dlt
        var = jnp.sum(sq, axis=-1, keepdims=True) / CONV_CH
        rstd = lax.rsqrt(var + EPS)
        for c in range(nch):
            cs = slice(c * LANES, (c + 1) * LANES)
            yn = (accs[c] - mu) * rstd * lng_ref[:, cs] + lnb_ref[:, cs]
            o_ref[0, r0:r0 + rs, cs] = (yn * jax.nn.sigmoid(yn)).astype(BF16)


def _conv_call(y, conv_w, conv_b, ln_g, ln_b, rc=256, rs=32):
    b, s, ch = y.shape
    hb = rc // HALO
    nh = s // HALO
    return pl.pallas_call(
        functools.partial(_conv_kernel, rc=rc, rs=rs),
        grid=(b, s // rc),
        in_specs=[
            pl.BlockSpec((1, rc, ch), lambda i, j: (i, j, 0)),
            pl.BlockSpec((1, HALO, ch), lambda i, j: (i, jnp.maximum(j * hb - 1, 0), 0)),
            pl.BlockSpec((1, HALO, ch), lambda i, j: (i, jnp.minimum((j + 1) * hb, nh - 1), 0)),
            pl.BlockSpec((CONV_WIDTH, ch), lambda i, j: (0, 0)),
            pl.BlockSpec((1, ch), lambda i, j: (0, 0)),
            pl.BlockSpec((1, ch), lambda i, j: (0, 0)),
            pl.BlockSpec((1, ch), lambda i, j: (0, 0)),
        ],
        out_specs=pl.BlockSpec((1, rc, ch), lambda i, j: (i, j, 0)),
        out_shape=jax.ShapeDtypeStruct((b, s, ch), BF16),
        scratch_shapes=[pltpu.VMEM((SUBLANES, rc + 2 * HALO, ch), F32)],
        compiler_params=_cparams(("parallel", "parallel"), 32),
        name="conv_group",
    )(y, y, y, conv_w, conv_b, ln_g, ln_b)


def _store_token_tiles(ref, lead, row0, val):
    r, d = val.shape
    nc = d // LANES
    for g in range(r // SUBLANES):
        for c in range(nc):
            dst = pl.ds((row0 + g * SUBLANES) * nc + c, SUBLANES, stride=nc)
            ref[lead + (dst, slice(None))] = val[g * SUBLANES:(g + 1) * SUBLANES, c * LANES:(c + 1) * LANES]


def _load_token_tiles(ref, lead, row0, r, d):
    nc = d // LANES
    groups = []
    for g in range(r // SUBLANES):
        groups.append(jnp.concatenate(
            [ref[lead + (pl.ds((row0 + g * SUBLANES) * nc + c, SUBLANES, stride=nc), slice(None))]
             for c in range(nc)], axis=-1))
    return jnp.concatenate(groups, axis=0)


def _outproj_kernel(a_ref, cv_ref, x_ref, mod_ref, wo_ref, g2_ref, wrt_ref, x1_ref, h2_ref, lg_ref, *, sub):
    half = a_ref.shape[-1]
    gate1 = mod_ref[0, 2:3, :]
    shift2 = mod_ref[0, 3:4, :]
    scale2 = mod_ref[0, 4:5, :]
    for sb in range(x_ref.shape[1] // sub):
        rows = slice(sb * sub, (sb + 1) * sub)
        mix = (jnp.dot(a_ref[0, rows, :], wo_ref[:half, :], preferred_element_type=F32)
               + jnp.dot(cv_ref[0, rows, :], wo_ref[half:, :], preferred_element_type=F32))
        x1 = x_ref[0, rows, :] + gate1 * mix
        x1_ref[0, rows, :] = x1
        h2 = x1 * lax.rsqrt(jnp.mean(x1 * x1, axis=-1, keepdims=True) + EPS) * g2_ref[...]
        h2 = h2 * (1.0 + scale2) + shift2
        _store_token_tiles(h2_ref, (0,), sb * sub, h2)
        lg_ref[0, :, rows] = lax.dot_general(wrt_ref[...], h2, (((1,), (1,)), ((), ())),
                                             precision=HIGHEST, preferred_element_type=F32)


def _outproj_call(attn, conv, x, mod_l, w_out_b, g2, w_router_t, tm=1024, sub=256):
    b, s, d = x.shape
    half = attn.shape[-1]
    ne = w_router_t.shape[0]
    return pl.pallas_call(
        functools.partial(_outproj_kernel, sub=sub),
        grid=(b, s // tm),
        in_specs=[
            pl.BlockSpec((1, tm, half), lambda i, j: (i, j, 0)),
            pl.BlockSpec((1, tm, half), lambda i, j: (i, j, 0)),
            pl.BlockSpec((1, tm, d), lambda i, j: (i, j, 0)),
            pl.BlockSpec((1, N_MOD, d), lambda i, j: (i, 0, 0)),
            pl.BlockSpec((2 * half, d), lambda i, j: (0, 0)),
            pl.BlockSpec((1, d), lambda i, j: (0, 0)),
            pl.BlockSpec((ne, d), lambda i, j: (0, 0)),
        ],
        out_specs=[
            pl.BlockSpec((1, tm, d), lambda i, j: (i, j, 0)),
            pl.BlockSpec((1, tm * d // LANES, LANES), lambda i, j: (i, j, 0)),
            pl.BlockSpec((1, ne, tm), lambda i, j: (i, 0, j)),
        ],
        out_shape=[
            jax.ShapeDtypeStruct((b, s, d), F32),
            jax.ShapeDtypeStruct((b, s * d // LANES, LANES), F32),
            jax.ShapeDtypeStruct((b, ne, s), F32),
        ],
        compiler_params=_cparams(("parallel", "parallel"), 48),
        name="out_proj",
    )(attn, conv, x, mod_l, w_out_b, g2, w_router_t)


def _prefix_lanes(xb):
    r, s = xb.shape
    row = lax.broadcasted_iota(jnp.int32, (LANES, LANES), 0)
    col = lax.broadcasted_iota(jnp.int32, (LANES, LANES), 1)
    tri = jnp.where(row <= col, 1.0, 0.0).astype(BF16)
    carry = jnp.zeros((r, 1), F32)
    outs = []
    for blk in range(s // LANES):
        pb = jnp.dot(xb[:, blk * LANES:(blk + 1) * LANES], tri, preferred_element_type=F32) + carry
        outs.append(pb)
        carry = pb[:, LANES - 1:LANES]
    return jnp.concatenate(outs, axis=-1)


def _route_kernel(lg_ref, idx_ref, gate_ref, *, cap, refine):
    lg = lg_ref[0]
    ne, s = lg.shape
    idx_bits = (s - 1).bit_length()
    m = jnp.max(lg, axis=0, keepdims=True)
    ex = jnp.exp(lg - m)
    aff = ex / jnp.sum(ex, axis=0, keepdims=True)

    def count_ge(thr):
        return jnp.sum(jnp.where(aff >= thr, 1.0, 0.0), axis=-1, keepdims=True)

    def enough(bits):
        return count_ge(pltpu.bitcast(bits, F32)) >= cap

    cur = jnp.zeros((ne, 1), jnp.int32)
    for bit in range(30, 0, -2):
        c_hi = cur | (1 << bit)
        c_lo = cur | (1 << (bit - 1))
        c_both = c_hi | (1 << (bit - 1))
        cur = jnp.where(enough(c_both), c_both, jnp.where(enough(c_hi), c_hi, jnp.where(enough(c_lo), c_lo, cur)))
    cur = jnp.where(enough(cur | 1), cur | 1, cur)
    lo = pltpu.bitcast(cur, F32)
    hi = pltpu.bitcast(cur + 1, F32)
    for _ in range(refine):
        t1 = lo + 0.25 * (hi - lo)
        t2 = lo + 0.5 * (hi - lo)
        t3 = lo + 0.75 * (hi - lo)
        e1 = count_ge(t1) >= cap
        e2 = count_ge(t2) >= cap
        e3 = count_ge(t3) >= cap
        lo, hi = (jnp.where(e3, t3, jnp.where(e2, t2, jnp.where(e1, t1, lo))),
                  jnp.where(e1, jnp.where(e2, jnp.where(e3, hi, t3), t2), t1))
    above = aff >= hi
    window = (aff >= lo) & jnp.logical_not(above)
    c_above = _prefix_lanes(jnp.where(above, 1.0, 0.0).astype(BF16))
    c_win = _prefix_lanes(jnp.where(window, 1.0, 0.0).astype(BF16))
    need = cap - c_above[:, -1:]
    sel = above | (window & (c_win <= need))
    rank = (c_above + jnp.minimum(c_win, need)).astype(jnp.int32) - 1
    t = lax.broadcasted_iota(jnp.int32, (ne, s), 1)
    w = jnp.where(sel, ((t - rank) << (idx_bits + 1)) | (t << 1) | 1, 0)
    g = aff
    for kk in range(idx_bits):
        step = 1 << kk
        w_in = pltpu.roll(w, s - step, 1)
        g_in = pltpu.roll(g, s - step, 1)
        take = ((w_in >> (idx_bits + 1 + kk)) & w_in & 1) == 1
        stay = ((w >> (idx_bits + 1 + kk)) & 1) == 0
        w = jnp.where(take, w_in, jnp.where(stay, w, 0))
        g = jnp.where(take, g_in, g)
    idx_ref[0] = (w[:, :cap] >> 1) & ((1 << idx_bits) - 1)
    gate_ref[0] = g[:, :cap]


def _route_call(logits_t, cap, refine=3):
    b, ne, s = logits_t.shape
    return pl.pallas_call(
        functools.partial(_route_kernel, cap=cap, refine=refine),
        grid=(b,),
        in_specs=[pl.BlockSpec((1, ne, s), lambda i: (i, 0, 0))],
        out_specs=[pl.BlockSpec((1, ne, cap), lambda i: (i, 0, 0))] * 2,
        out_shape=[jax.ShapeDtypeStruct((b, ne, cap), jnp.int32), jax.ShapeDtypeStruct((b, ne, cap), F32)],
        compiler_params=_cparams(("parallel",), 32),
        name="route",
    )(logits_t)


def _moe_kernel(idx_ref, gate_ref, h2_ref, wg_ref, wu_ref, wd_ref, acc_ref, xe_ref, ye_ref, *, cap, grp, d):
    e = pl.program_id(1)
    last = pl.num_programs(1) - 1
    nc = d // LANES
    slot = e % 2
    other = 1 - slot
    this_base = e * cap
    prev_base = jnp.maximum(e - 1, 0) * cap
    next_base = jnp.minimum(e + 1, last) * cap

    def token_rows(tok):
        return pl.ds(pl.multiple_of(tok * nc, nc), nc)

    def accumulate(base, y_slot, i):
        toks = [idx_ref[0, 0, base + i * grp + u] for u in range(grp)]
        new = [acc_ref[0, token_rows(toks[u]), :]
               + gate_ref[0, 0, base + i * grp + u] * ye_ref[y_slot, token_rows(i * grp + u), :] for u in range(grp)]
        for u in range(grp):
            acc_ref[0, token_rows(toks[u]), :] = new[u]

    @pl.when(e == 0)
    def _():
        acc_ref[...] = jnp.zeros(acc_ref.shape, F32)
        ye_ref[1] = jnp.zeros(ye_ref.shape[1:], F32)

        def gather(i, carry):
            for u in range(grp):
                j = i * grp + u
                xe_ref[0, token_rows(j), :] = h2_ref[0, token_rows(idx_ref[0, 0, j]), :]
            return carry

        lax.fori_loop(0, cap // grp, gather, 0)

    xe = _load_token_tiles(xe_ref, (slot,), 0, cap, d).astype(BF16)
    for i in range(cap // grp):
        accumulate(prev_base, other, i)
    for j in range(cap):
        xe_ref[other, token_rows(j), :] = h2_ref[0, token_rows(idx_ref[0, 0, next_base + j]), :]
    g = jnp.dot(xe, wg_ref[0, 0], preferred_element_type=F32)
    hid = (g * jax.nn.sigmoid(g)) * jnp.dot(xe, wu_ref[0, 0], preferred_element_type=F32)
    _store_token_tiles(ye_ref, (slot,), 0, jnp.dot(hid.astype(BF16), wd_ref[0, 0], preferred_element_type=F32))

    @pl.when(e == last)
    def _():
        def tail(i, carry):
            accumulate(this_base, slot, i)
            return carry

        lax.fori_loop(0, cap // grp, tail, 0)


def _moe_call(idx, gate, h2_tt, wg_b, wu_b, wd_b, layer, d, grp=8):
    b, rows, _ = h2_tt.shape
    ne, cap = idx.shape[1:]
    ff = wd_b.shape[2]
    nc = d // LANES
    per_sequence = pl.BlockSpec((1, 1, ne * cap), lambda i, e: (i, 0, 0), memory_space=pltpu.SMEM)
    return pl.pallas_call(
        functools.partial(_moe_kernel, cap=cap, grp=grp, d=d),
        grid=(b, ne),
        in_specs=[
            per_sequence, per_sequence,
            pl.BlockSpec((1, rows, LANES), lambda i, e: (i, 0, 0)),
            pl.BlockSpec((1, 1, d, ff), lambda i, e: (layer, e, 0, 0)),
            pl.BlockSpec((1, 1, d, ff), lambda i, e: (layer, e, 0, 0)),
            pl.BlockSpec((1, 1, ff, d), lambda i, e: (layer, e, 0, 0)),
        ],
        out_specs=pl.BlockSpec((1, rows, LANES), lambda i, e: (i, 0, 0)),
        out_shape=jax.ShapeDtypeStruct((b, rows, LANES), F32),
        scratch_shapes=[pltpu.VMEM((2, cap * nc, LANES), F32), pltpu.VMEM((2, cap * nc, LANES), F32)],
        compiler_params=_cparams(("arbitrary", "arbitrary"), 56),
        name="moe_experts",
    )(idx.reshape(b, 1, ne * cap), gate.reshape(b, 1, ne * cap), h2_tt, wg_b, wu_b, wd_b)


def _combine_kernel(x1_ref, acc_ref, mod_ref, o_ref, *, sub):
    gate2 = mod_ref[0, 5:6, :]
    d = x1_ref.shape[-1]
    for sb in range(x1_ref.shape[1] // sub):
        rows = slice(sb * sub, (sb + 1) * sub)
        o_ref[0, rows, :] = x1_ref[0, rows, :] + gate2 * _load_token_tiles(acc_ref, (0,), sb * sub, sub, d)


def _combine_call(x1, acc_tt, mod_l, tm=512, sub=128):
    b, s, d = x1.shape
    return pl.pallas_call(
        functools.partial(_combine_kernel, sub=sub),
        grid=(b, s // tm),
        in_specs=[
            pl.BlockSpec((1, tm, d), lambda i, j: (i, j, 0)),
            pl.BlockSpec((1, tm * d // LANES, LANES), lambda i, j: (i, j, 0)),
            pl.BlockSpec((1, N_MOD, d), lambda i, j: (i, 0, 0)),
        ],
        out_specs=pl.BlockSpec((1, tm, d), lambda i, j: (i, j, 0)),
        out_shape=jax.ShapeDtypeStruct((b, s, d), F32),
        compiler_params=_cparams(("parallel", "parallel"), 32),
        name="moe_combine",
    )(x1, acc_tt, mod_l)


def _permute_weights(w_in, w_uq, w_ukv):
    o1 = Q_RANK
    o2 = o1 + KV_RANK
    o3 = o2 + ROPE
    w_in_p = jnp.concatenate([w_in[..., :o2], w_in[..., o3:], w_in[..., o2:o3]], axis=-1)
    depth, qr, _ = w_uq.shape
    wq = w_uq.reshape(depth, qr, N_HEADS, QK_DIM)
    w_uq_p = jnp.concatenate([wq[..., :NOPE].reshape(depth, qr, N_HEADS * NOPE),
                              wq[..., NOPE:].reshape(depth, qr, N_HEADS * ROPE)], axis=-1)
    kr = w_ukv.shape[1]
    wkv = w_ukv.reshape(depth, kr, N_HEADS, NOPE + V_DIM)
    w_ukv_p = jnp.concatenate([wkv[..., :NOPE].reshape(depth, kr, N_HEADS * NOPE),
                               wkv[..., NOPE:].reshape(depth, kr, N_HEADS * V_DIM)], axis=-1)
    return w_in_p.astype(BF16), w_uq_p.astype(BF16), w_ukv_p.astype(BF16)


def kernel(x, c, positions, norm1_g, w_ada, b_ada, w_in, q_latent_g, w_uq, kv_latent_g, w_ukv, q_head_g, k_head_g, conv_w, conv_b, conv_norm_g, conv_norm_b, w_out, norm2_g, w_router, w_gate, w_up, w_down):
    b, s, d = x.shape
    depth = w_ada.shape[0]
    cap = 2 * s // N_EXPERTS

    mod = _ada_call(c, w_ada, b_ada).reshape(depth, b, N_MOD, d)
    cos4, sin4 = _rope_tables(positions)
    w_in_p, w_uq_p, w_ukv_p = _permute_weights(w_in, w_uq, w_ukv)
    w_out_b = w_out.astype(BF16)
    wg_b = w_gate.astype(BF16)
    wu_b = w_up.astype(BF16)
    wd_b = w_down.astype(BF16)
    w_router_t = jnp.swapaxes(w_router, 1, 2)

    for l in range(depth):
        gqn = q_head_g[l, :NOPE].reshape(1, NOPE)
        gqr2 = jnp.tile(q_head_g[l, NOPE:], 2).reshape(1, LANES)
        gkn = k_head_g[l, :NOPE].reshape(1, NOPE)
        gkr2 = jnp.tile(k_head_g[l, NOPE:], 2).reshape(1, LANES)
        q, k, v, y = _inproj_call(x, mod[l], norm1_g[l].reshape(1, d), w_in_p[l],
                                  q_latent_g[l].reshape(1, Q_RANK), w_uq_p[l],
                                  kv_latent_g[l].reshape(1, KV_RANK), w_ukv_p[l],
                                  gqn, gqr2, gkn, gkr2, cos4, sin4)
        attn = _attn_call(q, k, v)
        conv = _conv_call(y, conv_w[l], conv_b[l].reshape(1, CONV_CH),
                          conv_norm_g[l].reshape(1, CONV_CH), conv_norm_b[l].reshape(1, CONV_CH))
        x1, h2_tt, logits_t = _outproj_call(attn, conv, x, mod[l], w_out_b[l],
                                            norm2_g[l].reshape(1, d), w_router_t[l])
        idx, gate = _route_call(logits_t, cap)
        acc_tt = _moe_call(idx, gate, h2_tt, wg_b, wu_b, wd_b, l, d)
        x = _combine_call(x1, acc_tt, mod[l])
    return x
```

```python
import functools
import math

import jax
import jax.numpy as jnp
from jax import lax
from jax.experimental import pallas as pl
from jax.experimental.pallas import tpu as pltpu

F32 = jnp.float32
BF16 = jnp.bfloat16

D_MODEL = 1024
N_HEADS = 4
NOPE = 128
ROPE = 64
QK_DIM = NOPE + ROPE
V_DIM = 128
Q_RANK = 256
KV_RANK = 128
CONV_CH = 512
CONV_WIDTH = 31
CONV_PAD = CONV_WIDTH // 2
N_EXPERTS = 16
EXPERT_FF = 1024
N_MOD = 6
ROPE_THETA = 10000.0
EPS = 1e-6

LANES = 128
SUBLANES = 8
HALO = 16
MIB = 1024 * 1024


def _cparams(sem, vmem_mib):
    return pltpu.CompilerParams(dimension_semantics=sem, vmem_limit_bytes=vmem_mib * MIB)


def _ada_kernel(c_ref, w_ref, b_ref, o_ref):
    c = c_ref[...]
    ca = c * jax.nn.sigmoid(c)
    ca_hi = ca.astype(BF16)
    ca_lo = (ca - ca_hi.astype(F32)).astype(BF16)
    w = w_ref[0]
    w_hi = w.astype(BF16)
    w_lo = (w - w_hi.astype(F32)).astype(BF16)
    nb = c.shape[0]
    by_hi = jnp.dot(jnp.concatenate([ca_hi, ca_lo], axis=0), w_hi, preferred_element_type=F32)
    o_ref[0] = by_hi[:nb] + by_hi[nb:] + jnp.dot(ca_hi, w_lo, preferred_element_type=F32) + b_ref[0]


def _ada_call(c, w_ada, b_ada):
    depth, d, n = w_ada.shape
    b = c.shape[0]
    tn = 1536
    return pl.pallas_call(
        _ada_kernel,
        grid=(depth, n // tn),
        in_specs=[
            pl.BlockSpec((b, d), lambda l, j: (0, 0)),
            pl.BlockSpec((1, d, tn), lambda l, j: (l, 0, j)),
            pl.BlockSpec((1, 1, tn), lambda l, j: (l, 0, j)),
        ],
        out_specs=pl.BlockSpec((1, b, tn), lambda l, j: (l, 0, j)),
        out_shape=jax.ShapeDtypeStruct((depth, b, n), F32),
        compiler_params=_cparams(("parallel", "parallel"), 40),
        name="adaln_mod",
    )(c, w_ada, b_ada.reshape(depth, 1, n))


def _rope_kernel(pos_ref, invf_ref, cos_ref, sin_ref):
    half = ROPE // 2
    per_row = LANES // half
    ang = pos_ref[0].astype(F32) * invf_ref[...]
    packed_cos = jnp.cos(ang)
    packed_sin = jnp.sin(ang)
    n = ang.shape[0]
    group = lax.broadcasted_iota(jnp.int32, ang.shape, 1) // half
    sign = jnp.where(group % 2 == 0, -1.0, 1.0)

    def spread(p, j):
        out = p
        for q in range(per_row):
            if q != j:
                out = jnp.where(group == q, pltpu.roll(p, (half * (q - j)) % LANES, 1), out)
        return out

    for j in range(per_row):
        cos_ref[0, pl.ds(j, n, stride=per_row), :] = spread(packed_cos, j)
        sin_ref[0, pl.ds(j, n, stride=per_row), :] = spread(packed_sin, j) * sign


def _rope_tables(positions):
    b, s = positions.shape
    half = ROPE // 2
    per_row = LANES // half
    inv_freq = 1.0 / (ROPE_THETA ** (jnp.arange(0, ROPE, 2, dtype=F32) / ROPE))
    invf = jnp.tile(inv_freq, per_row).reshape(1, LANES)
    pos_rep = jnp.repeat(positions.reshape(b, s // per_row, per_row), half, axis=-1)
    return pl.pallas_call(
        _rope_kernel,
        grid=(b,),
        in_specs=[
            pl.BlockSpec((1, s // per_row, LANES), lambda i: (i, 0, 0)),
            pl.BlockSpec((1, LANES), lambda i: (0, 0)),
        ],
        out_specs=[pl.BlockSpec((1, s, LANES), lambda i: (i, 0, 0))] * 2,
        out_shape=[jax.ShapeDtypeStruct((b, s, LANES), F32)] * 2,
        compiler_params=_cparams(("parallel",), 32),
        name="rope_tables",
    )(pos_rep, invf)


def _inproj_kernel(x_ref, mod_ref, g1_ref, win_ref, qlg_ref, wuq_ref, kvlg_ref, wukv_ref,
                   gqn_ref, gqr_ref, gkn_ref, gkr_ref, cos_ref, sin_ref,
                   q_ref, k_ref, v_ref, y_ref, *, sub):
    for sb in range(x_ref.shape[1] // sub):
        _inproj_rows(slice(sb * sub, (sb + 1) * sub), x_ref, mod_ref, g1_ref, win_ref, qlg_ref, wuq_ref, kvlg_ref,
                     wukv_ref, gqn_ref, gqr_ref, gkn_ref, gkr_ref, cos_ref, sin_ref, q_ref, k_ref, v_ref, y_ref)


def _inproj_rows(rows, x_ref, mod_ref, g1_ref, win_ref, qlg_ref, wuq_ref, kvlg_ref, wukv_ref,
                 gqn_ref, gqr_ref, gkn_ref, gkr_ref, cos_ref, sin_ref, q_ref, k_ref, v_ref, y_ref):
    x = x_ref[0, rows, :]
    tm = x.shape[0]
    shift1 = mod_ref[0, 0:1, :]
    scale1 = mod_ref[0, 1:2, :]
    h = x * lax.rsqrt(jnp.mean(x * x, axis=-1, keepdims=True) + EPS) * g1_ref[...]
    h = h * (1.0 + scale1) + shift1
    proj = jnp.dot(h.astype(BF16), win_ref[...], preferred_element_type=F32)
    o1 = Q_RANK
    o2 = o1 + KV_RANK
    o3 = o2 + CONV_CH
    o4 = o3 + CONV_CH
    c_q = proj[:, :o1]
    c_kv = proj[:, o1:o2]
    y_ref[0, rows, :] = proj[:, o2:o3] * jax.nn.sigmoid(proj[:, o3:o4])
    kr = proj[:, o4:]

    cqn = c_q * lax.rsqrt(jnp.mean(c_q * c_q, axis=-1, keepdims=True) + EPS) * qlg_ref[...]
    qf = jnp.dot(cqn.astype(BF16), wuq_ref[...], preferred_element_type=F32)
    ckn = c_kv * lax.rsqrt(jnp.mean(c_kv * c_kv, axis=-1, keepdims=True) + EPS) * kvlg_ref[...]
    kvf = jnp.dot(ckn.astype(BF16), wukv_ref[...], preferred_element_type=F32)

    cos4 = cos_ref[0, rows, :]
    sin4 = sin_ref[0, rows, :]
    lane = lax.broadcasted_iota(jnp.int32, (tm, LANES), 1)
    first_half = (lane & (ROPE // 2)) == 0

    def rope_pair(xb):
        swapped = jnp.where(first_half, pltpu.roll(xb, LANES - ROPE // 2, 1), pltpu.roll(xb, ROPE // 2, 1))
        return xb * cos4 + swapped * sin4

    kr_rot = rope_pair(jnp.concatenate([kr, kr], axis=-1) * gkr_ref[...])[:, :ROPE]
    kr_ss = jnp.sum(kr * kr, axis=-1, keepdims=True)
    nn = N_HEADS * NOPE
    for hh in range(N_HEADS):
        kn = kvf[:, hh * NOPE:(hh + 1) * NOPE]
        ss = jnp.sum(kn * kn, axis=-1, keepdims=True) + kr_ss
        r = lax.rsqrt(ss / QK_DIM + EPS)
        k_ref[0, hh, rows, :NOPE] = (kn * r * gkn_ref[...]).astype(BF16)
        k_ref[0, hh, rows, NOPE:] = (kr_rot * r).astype(BF16)
        v_ref[0, hh, rows, :] = kvf[:, nn + hh * V_DIM:nn + (hh + 1) * V_DIM].astype(BF16)

    qscale = math.log2(math.e) / math.sqrt(QK_DIM)
    for pair in range(N_HEADS // 2):
        qr = qf[:, nn + pair * LANES:nn + (pair + 1) * LANES]
        qr_rot = rope_pair(qr * gqr_ref[...])
        qr_sq = qr * qr
        for j in range(2):
            hh = 2 * pair + j
            qn = qf[:, hh * NOPE:(hh + 1) * NOPE]
            ss = (jnp.sum(qn * qn, axis=-1, keepdims=True)
                  + jnp.sum(qr_sq[:, j * ROPE:(j + 1) * ROPE], axis=-1, keepdims=True))
            r = lax.rsqrt(ss / QK_DIM + EPS) * qscale
            q_ref[0, hh, rows, :NOPE] = (qn * r * gqn_ref[...]).astype(BF16)
            q_ref[0, hh, rows, NOPE:] = (qr_rot[:, j * ROPE:(j + 1) * ROPE] * r).astype(BF16)


def _inproj_call(x, mod_l, g1, w_in_p, qlg, w_uq_p, kvlg, w_ukv_p, gqn, gqr2, gkn, gkr2, cos4, sin4, tm=1024, sub=256):
    b, s, d = x.shape
    full = lambda shape: pl.BlockSpec(shape, lambda i, j: (0,) * len(shape))
    return pl.pallas_call(
        functools.partial(_inproj_kernel, sub=sub),
        grid=(b, s // tm),
        in_specs=[
            pl.BlockSpec((1, tm, d), lambda i, j: (i, j, 0)),
            pl.BlockSpec((1, N_MOD, d), lambda i, j: (i, 0, 0)),
            full((1, d)),
            full(w_in_p.shape),
            full((1, Q_RANK)),
            full(w_uq_p.shape),
            full((1, KV_RANK)),
            full(w_ukv_p.shape),
            full((1, NOPE)),
            full((1, LANES)),
            full((1, NOPE)),
            full((1, LANES)),
            pl.BlockSpec((1, tm, LANES), lambda i, j: (i, j, 0)),
            pl.BlockSpec((1, tm, LANES), lambda i, j: (i, j, 0)),
        ],
        out_specs=[
            pl.BlockSpec((1, N_HEADS, tm, QK_DIM), lambda i, j: (i, 0, j, 0)),
            pl.BlockSpec((1, N_HEADS, tm, QK_DIM), lambda i, j: (i, 0, j, 0)),
            pl.BlockSpec((1, N_HEADS, tm, V_DIM), lambda i, j: (i, 0, j, 0)),
            pl.BlockSpec((1, tm, CONV_CH), lambda i, j: (i, j, 0)),
        ],
        out_shape=[
            jax.ShapeDtypeStruct((b, N_HEADS, s, QK_DIM), BF16),
            jax.ShapeDtypeStruct((b, N_HEADS, s, QK_DIM), BF16),
            jax.ShapeDtypeStruct((b, N_HEADS, s, V_DIM), BF16),
            jax.ShapeDtypeStruct((b, s, CONV_CH), F32),
        ],
        compiler_params=_cparams(("parallel", "parallel"), 48),
        name="in_proj",
    )(x, mod_l, g1, w_in_p, qlg, w_uq_p, kvlg, w_ukv_p, gqn, gqr2, gkn, gkr2, cos4, sin4)


def _attn_kernel(q_ref, k_ref, v_ref, o_ref, *, sub):
    dv = v_ref.shape[-1]
    for hh in range(q_ref.shape[1]):
        k = k_ref[0, hh]
        v = v_ref[0, hh]
        for sb in range(q_ref.shape[2] // sub):
            rows = slice(sb * sub, (sb + 1) * sub)
            s2 = lax.dot_general(q_ref[0, hh, rows, :], k, (((1,), (1,)), ((), ())), preferred_element_type=F32)
            m = jnp.max(s2, axis=-1, keepdims=True)
            p = jnp.exp2(s2 - m)
            l = jnp.sum(p, axis=-1, keepdims=True)
            o = jnp.dot(p.astype(BF16), v, preferred_element_type=F32)
            o_ref[0, rows, hh * dv:(hh + 1) * dv] = (o / l).astype(BF16)


def _attn_call(q, k, v, hp=2, sub=512):
    b, nh, s, dk = q.shape
    dv = v.shape[-1]
    return pl.pallas_call(
        functools.partial(_attn_kernel, sub=sub),
        grid=(b, nh // hp),
        in_specs=[
            pl.BlockSpec((1, hp, s, dk), lambda i, h: (i, h, 0, 0)),
            pl.BlockSpec((1, hp, s, dk), lambda i, h: (i, h, 0, 0)),
            pl.BlockSpec((1, hp, s, dv), lambda i, h: (i, h, 0, 0)),
        ],
        out_specs=pl.BlockSpec((1, s, hp * dv), lambda i, h: (i, 0, h)),
        out_shape=jax.ShapeDtypeStruct((b, s, nh * dv), BF16),
        compiler_params=_cparams(("parallel", "parallel"), 48),
        name="attention",
    )(q, k, v)


def _conv_kernel(yc_ref, yp_ref, yn_ref, w_ref, cb_ref, lng_ref, lnb_ref, o_ref, win_ref, *, rc, rs):
    i = pl.program_id(1)
    nblk = pl.num_programs(1)
    win_ref[0, 0:HALO, :] = jnp.where(i > 0, yp_ref[0], 0.0)
    win_ref[0, HALO:HALO + rc, :] = yc_ref[0]
    win_ref[0, HALO + rc:HALO + rc + HALO, :] = jnp.where(i < nblk - 1, yn_ref[0], 0.0)
    span = rc + HALO + SUBLANES
    for sh in range(1, SUBLANES):
        win_ref[sh, 0:span, :] = win_ref[0, sh:sh + span, :]
    nch = CONV_CH // LANES
    for sub in range(rc // rs):
        r0 = sub * rs
        accs = [jnp.zeros((rs, LANES), F32) + cb_ref[:, c * LANES:(c + 1) * LANES] for c in range(nch)]
        for kk in range(CONV_WIDTH):
            off = HALO - CONV_PAD + kk
            base = r0 + off - off % SUBLANES
            for c in range(nch):
                cs = slice(c * LANES, (c + 1) * LANES)
                accs[c] = accs[c] + win_ref[off % SUBLANES, base:base + rs, cs] * w_ref[kk:kk + 1, cs]
        tot = accs[0]
        for c in range(1, nch):
            tot = tot + accs[c]
        mu = jnp.sum(tot, axis=-1, keepdims=True) / CONV_CH
        sq = jnp.zeros((rs, LANES), F32)
        for c in range(nch):
            dlt = accs[c] - mu
            sq = sq + dlt * dlt
        var = jnp.sum(sq, axis=-1, keepdims=True) / CONV_CH
        rstd = lax.rsqrt(var + EPS)
        for c in range(nch):
            cs = slice(c * LANES, (c + 1) * LANES)
            yn = (accs[c] - mu) * rstd * lng_ref[:, cs] + lnb_ref[:, cs]
            o_ref[0, r0:r0 + rs, cs] = (yn * jax.nn.sigmoid(yn)).astype(BF16)


def _conv_call(y, conv_w, conv_b, ln_g, ln_b, rc=512, rs=32):
    b, s, ch = y.shape
    hb = rc // HALO
    nh = s // HALO
    return pl.pallas_call(
        functools.partial(_conv_kernel, rc=rc, rs=rs),
        grid=(b, s // rc),
        in_specs=[
            pl.BlockSpec((1, rc, ch), lambda i, j: (i, j, 0)),
            pl.BlockSpec((1, HALO, ch), lambda i, j: (i, jnp.maximum(j * hb - 1, 0), 0)),
            pl.BlockSpec((1, HALO, ch), lambda i, j: (i, jnp.minimum((j + 1) * hb, nh - 1), 0)),
            pl.BlockSpec((CONV_WIDTH, ch), lambda i, j: (0, 0)),
            pl.BlockSpec((1, ch), lambda i, j: (0, 0)),
            pl.BlockSpec((1, ch), lambda i, j: (0, 0)),
            pl.BlockSpec((1, ch), lambda i, j: (0, 0)),
        ],
        out_specs=pl.BlockSpec((1, rc, ch), lambda i, j: (i, j, 0)),
        out_shape=jax.ShapeDtypeStruct((b, s, ch), BF16),
        scratch_shapes=[pltpu.VMEM((SUBLANES, rc + 2 * HALO, ch), F32)],
        compiler_params=_cparams(("parallel", "parallel"), 32),
        name="conv_group",
    )(y, y, y, conv_w, conv_b, ln_g, ln_b)


def _store_token_tiles(ref, lead, row0, val):
    r, d = val.shape
    nc = d // LANES
    for g in range(r // SUBLANES):
        for c in range(nc):
            dst = pl.ds((row0 + g * SUBLANES) * nc + c, SUBLANES, stride=nc)
            ref[lead + (dst, slice(None))] = val[g * SUBLANES:(g + 1) * SUBLANES, c * LANES:(c + 1) * LANES]


def _load_token_tiles(ref, lead, row0, r, d):
    nc = d // LANES
    groups = []
    for g in range(r // SUBLANES):
        groups.append(jnp.concatenate(
            [ref[lead + (pl.ds((row0 + g * SUBLANES) * nc + c, SUBLANES, stride=nc), slice(None))]
             for c in range(nc)], axis=-1))
    return jnp.concatenate(groups, axis=0)


def _outproj_kernel(a_ref, cv_ref, x_ref, mod_ref, wo_ref, g2_ref, wrt_ref, x1_ref, h2_ref, lg_ref, *, sub):
    half = a_ref.shape[-1]
    ne = lg_ref.shape[1]
    gate1 = mod_ref[0, 2:3, :]
    shift2 = mod_ref[0, 3:4, :]
    scale2 = mod_ref[0, 4:5, :]
    for sb in range(x_ref.shape[1] // sub):
        rows = slice(sb * sub, (sb + 1) * sub)
        mix = (jnp.dot(a_ref[0, rows, :], wo_ref[:half, :], preferred_element_type=F32)
               + jnp.dot(cv_ref[0, rows, :], wo_ref[half:, :], preferred_element_type=F32))
        x1 = x_ref[0, rows, :] + gate1 * mix
        x1_ref[0, rows, :] = x1
        h2 = x1 * lax.rsqrt(jnp.mean(x1 * x1, axis=-1, keepdims=True) + EPS) * g2_ref[...]
        h2 = h2 * (1.0 + scale2) + shift2
        _store_token_tiles(h2_ref, (0,), sb * sub, h2)
        h2_hi = h2.astype(BF16)
        h2_lo = (h2 - h2_hi.astype(F32)).astype(BF16)
        nt = (((1,), (1,)), ((), ()))
        by_hi = lax.dot_general(wrt_ref[...], h2_hi, nt, preferred_element_type=F32)
        by_lo = lax.dot_general(wrt_ref[:ne, :], h2_lo, nt, preferred_element_type=F32)
        lg_ref[0, :, rows] = by_hi[:ne] + by_hi[ne:] + by_lo


def _outproj_call(attn, conv, x, mod_l, w_out_b, g2, w_router_split, tm=1024, sub=256):
    b, s, d = x.shape
    half = attn.shape[-1]
    ne = w_router_split.shape[0] // 2
    return pl.pallas_call(
        functools.partial(_outproj_kernel, sub=sub),
        grid=(b, s // tm),
        in_specs=[
            pl.BlockSpec((1, tm, half), lambda i, j: (i, j, 0)),
            pl.BlockSpec((1, tm, half), lambda i, j: (i, j, 0)),
            pl.BlockSpec((1, tm, d), lambda i, j: (i, j, 0)),
            pl.BlockSpec((1, N_MOD, d), lambda i, j: (i, 0, 0)),
            pl.BlockSpec((2 * half, d), lambda i, j: (0, 0)),
            pl.BlockSpec((1, d), lambda i, j: (0, 0)),
            pl.BlockSpec((2 * ne, d), lambda i, j: (0, 0)),
        ],
        out_specs=[
            pl.BlockSpec((1, tm, d), lambda i, j: (i, j, 0)),
            pl.BlockSpec((1, tm * d // LANES, LANES), lambda i, j: (i, j, 0)),
            pl.BlockSpec((1, ne, tm), lambda i, j: (i, 0, j)),
        ],
        out_shape=[
            jax.ShapeDtypeStruct((b, s, d), F32),
            jax.ShapeDtypeStruct((b, s * d // LANES, LANES), F32),
            jax.ShapeDtypeStruct((b, ne, s), F32),
        ],
        compiler_params=_cparams(("parallel", "parallel"), 48),
        name="out_proj",
    )(attn, conv, x, mod_l, w_out_b, g2, w_router_split)


def _prefix_lanes(xb):
    r, s = xb.shape
    row = lax.broadcasted_iota(jnp.int32, (LANES, LANES), 0)
    col = lax.broadcasted_iota(jnp.int32, (LANES, LANES), 1)
    tri = jnp.where(row <= col, 1.0, 0.0).astype(BF16)
    carry = jnp.zeros((r, 1), F32)
    outs = []
    for blk in range(s // LANES):
        pb = jnp.dot(xb[:, blk * LANES:(blk + 1) * LANES], tri, preferred_element_type=F32) + carry
        outs.append(pb)
        carry = pb[:, LANES - 1:LANES]
    return jnp.concatenate(outs, axis=-1)


def _route_kernel(lg_ref, idx_ref, gate_ref, *, cap, refine):
    lg = lg_ref[0]
    ne, s = lg.shape
    idx_bits = (s - 1).bit_length()
    m = jnp.max(lg, axis=0, keepdims=True)
    ex = jnp.exp(lg - m)
    aff = ex / jnp.sum(ex, axis=0, keepdims=True)

    def count_ge(thr):
        return jnp.sum(jnp.where(aff >= thr, 1.0, 0.0), axis=-1, keepdims=True)

    def enough(bits):
        return count_ge(pltpu.bitcast(bits, F32)) >= cap

    cur = jnp.zeros((ne, 1), jnp.int32)
    for bit in range(30, 0, -2):
        c_hi = cur | (1 << bit)
        c_lo = cur | (1 << (bit - 1))
        c_both = c_hi | (1 << (bit - 1))
        cur = jnp.where(enough(c_both), c_both, jnp.where(enough(c_hi), c_hi, jnp.where(enough(c_lo), c_lo, cur)))
    cur = jnp.where(enough(cur | 1), cur | 1, cur)
    lo = pltpu.bitcast(cur, F32)
    hi = pltpu.bitcast(cur + 1, F32)
    for _ in range(refine):
        t1 = lo + 0.25 * (hi - lo)
        t2 = lo + 0.5 * (hi - lo)
        t3 = lo + 0.75 * (hi - lo)
        e1 = count_ge(t1) >= cap
        e2 = count_ge(t2) >= cap
        e3 = count_ge(t3) >= cap
        lo, hi = (jnp.where(e3, t3, jnp.where(e2, t2, jnp.where(e1, t1, lo))),
                  jnp.where(e1, jnp.where(e2, jnp.where(e3, hi, t3), t2), t1))
    above = aff >= hi
    window = (aff >= lo) & jnp.logical_not(above)
    c_above = _prefix_lanes(jnp.where(above, 1.0, 0.0).astype(BF16))
    c_win = _prefix_lanes(jnp.where(window, 1.0, 0.0).astype(BF16))
    need = cap - c_above[:, -1:]
    sel = above | (window & (c_win <= need))
    rank = (c_above + jnp.minimum(c_win, need)).astype(jnp.int32) - 1
    t = lax.broadcasted_iota(jnp.int32, (ne, s), 1)
    w = jnp.where(sel, ((t - rank) << (idx_bits + 1)) | (t << 1) | 1, 0)
    g = aff
    for kk in range(idx_bits):
        step = 1 << kk
        w_in = pltpu.roll(w, s - step, 1)
        g_in = pltpu.roll(g, s - step, 1)
        take = ((w_in >> (idx_bits + 1 + kk)) & w_in & 1) == 1
        stay = ((w >> (idx_bits + 1 + kk)) & 1) == 0
        w = jnp.where(take, w_in, jnp.where(stay, w, 0))
        g = jnp.where(take, g_in, g)
    idx_ref[0] = (w[:, :cap] >> 1) & ((1 << idx_bits) - 1)
    gate_ref[0] = g[:, :cap]


def _route_call(logits_t, cap, refine=3):
    b, ne, s = logits_t.shape
    return pl.pallas_call(
        functools.partial(_route_kernel, cap=cap, refine=refine),
        grid=(b,),
        in_specs=[pl.BlockSpec((1, ne, s), lambda i: (i, 0, 0))],
        out_specs=[pl.BlockSpec((1, ne, cap), lambda i: (i, 0, 0))] * 2,
        out_shape=[jax.ShapeDtypeStruct((b, ne, cap), jnp.int32), jax.ShapeDtypeStruct((b, ne, cap), F32)],
        compiler_params=_cparams(("parallel",), 32),
        name="route",
    )(logits_t)


def _moe_kernel(idx_ref, gate_ref, h2_ref, wg_ref, wu_ref, wd_ref, acc_ref, xe_ref, ye_ref, *, cap, grp, d):
    e = pl.program_id(1)
    last = pl.num_programs(1) - 1
    nc = d // LANES
    slot = e % 2
    other = 1 - slot
    this_base = e * cap
    prev_base = jnp.maximum(e - 1, 0) * cap
    next_base = jnp.minimum(e + 1, last) * cap

    def token_rows(tok):
        return pl.ds(pl.multiple_of(tok * nc, nc), nc)

    def accumulate(base, y_slot, i):
        toks = [idx_ref[0, 0, base + i * grp + u] for u in range(grp)]
        new = [acc_ref[0, token_rows(toks[u]), :]
               + gate_ref[0, 0, base + i * grp + u] * ye_ref[y_slot, token_rows(i * grp + u), :] for u in range(grp)]
        for u in range(grp):
            acc_ref[0, token_rows(toks[u]), :] = new[u]

    @pl.when(e == 0)
    def _():
        acc_ref[...] = jnp.zeros(acc_ref.shape, F32)
        ye_ref[1] = jnp.zeros(ye_ref.shape[1:], F32)

        def gather(i, carry):
            for u in range(grp):
                j = i * grp + u
                xe_ref[0, token_rows(j), :] = h2_ref[0, token_rows(idx_ref[0, 0, j]), :]
            return carry

        lax.fori_loop(0, cap // grp, gather, 0)

    xe = _load_token_tiles(xe_ref, (slot,), 0, cap, d).astype(BF16)
    for i in range(cap // grp):
        accumulate(prev_base, other, i)
    for j in range(cap):
        xe_ref[other, token_rows(j), :] = h2_ref[0, token_rows(idx_ref[0, 0, next_base + j]), :]
    g = jnp.dot(xe, wg_ref[0, 0], preferred_element_type=F32)
    hid = (g * jax.nn.sigmoid(g)) * jnp.dot(xe, wu_ref[0, 0], preferred_element_type=F32)
    _store_token_tiles(ye_ref, (slot,), 0, jnp.dot(hid.astype(BF16), wd_ref[0, 0], preferred_element_type=F32))

    @pl.when(e == last)
    def _():
        def tail(i, carry):
            accumulate(this_base, slot, i)
            return carry

        lax.fori_loop(0, cap // grp, tail, 0)


def _moe_call(idx, gate, h2_tt, wg_b, wu_b, wd_b, layer, d, grp=8):
    b, rows, _ = h2_tt.shape
    ne, cap = idx.shape[1:]
    ff = wd_b.shape[2]
    nc = d // LANES
    per_sequence = pl.BlockSpec((1, 1, ne * cap), lambda i, e: (i, 0, 0), memory_space=pltpu.SMEM)
    return pl.pallas_call(
        functools.partial(_moe_kernel, cap=cap, grp=grp, d=d),
        grid=(b, ne),
        in_specs=[
            per_sequence, per_sequence,
            pl.BlockSpec((1, rows, LANES), lambda i, e: (i, 0, 0)),
            pl.BlockSpec((1, 1, d, ff), lambda i, e: (layer, e, 0, 0)),
            pl.BlockSpec((1, 1, d, ff), lambda i, e: (layer, e, 0, 0)),
            pl.BlockSpec((1, 1, ff, d), lambda i, e: (layer, e, 0, 0)),
        ],
        out_specs=pl.BlockSpec((1, rows, LANES), lambda i, e: (i, 0, 0)),
        out_shape=jax.ShapeDtypeStruct((b, rows, LANES), F32),
        scratch_shapes=[pltpu.VMEM((2, cap * nc, LANES), F32), pltpu.VMEM((2, cap * nc, LANES), F32)],
        compiler_params=_cparams(("arbitrary", "arbitrary"), 56),
        name="moe_experts",
    )(idx.reshape(b, 1, ne * cap), gate.reshape(b, 1, ne * cap), h2_tt, wg_b, wu_b, wd_b)


def _combine_kernel(x1_ref, acc_ref, mod_ref, o_ref, *, sub):
    gate2 = mod_ref[0, 5:6, :]
    d = x1_ref.shape[-1]
    for sb in range(x1_ref.shape[1] // sub):
        rows = slice(sb * sub, (sb + 1) * sub)
        o_ref[0, rows, :] = x1_ref[0, rows, :] + gate2 * _load_token_tiles(acc_ref, (0,), sb * sub, sub, d)


def _combine_call(x1, acc_tt, mod_l, tm=512, sub=128):
    b, s, d = x1.shape
    return pl.pallas_call(
        functools.partial(_combine_kernel, sub=sub),
        grid=(b, s // tm),
        in_specs=[
            pl.BlockSpec((1, tm, d), lambda i, j: (i, j, 0)),
            pl.BlockSpec((1, tm * d // LANES, LANES), lambda i, j: (i, j, 0)),
            pl.BlockSpec((1, N_MOD, d), lambda i, j: (i, 0, 0)),
        ],
        out_specs=pl.BlockSpec((1, tm, d), lambda i, j: (i, j, 0)),
        out_shape=jax.ShapeDtypeStruct((b, s, d), F32),
        compiler_params=_cparams(("parallel", "parallel"), 32),
        name="moe_combine",
    )(x1, acc_tt, mod_l)


def _permute_weights(w_in, w_uq, w_ukv):
    o1 = Q_RANK
    o2 = o1 + KV_RANK
    o3 = o2 + ROPE
    w_in_p = jnp.concatenate([w_in[..., :o2], w_in[..., o3:], w_in[..., o2:o3]], axis=-1)
    depth, qr, _ = w_uq.shape
    wq = w_uq.reshape(depth, qr, N_HEADS, QK_DIM)
    w_uq_p = jnp.concatenate([wq[..., :NOPE].reshape(depth, qr, N_HEADS * NOPE),
                              wq[..., NOPE:].reshape(depth, qr, N_HEADS * ROPE)], axis=-1)
    kr = w_ukv.shape[1]
    wkv = w_ukv.reshape(depth, kr, N_HEADS, NOPE + V_DIM)
    w_ukv_p = jnp.concatenate([wkv[..., :NOPE].reshape(depth, kr, N_HEADS * NOPE),
                               wkv[..., NOPE:].reshape(depth, kr, N_HEADS * V_DIM)], axis=-1)
    return w_in_p.astype(BF16), w_uq_p.astype(BF16), w_ukv_p.astype(BF16)


def kernel(x, c, positions, norm1_g, w_ada, b_ada, w_in, q_latent_g, w_uq, kv_latent_g, w_ukv, q_head_g, k_head_g, conv_w, conv_b, conv_norm_g, conv_norm_b, w_out, norm2_g, w_router, w_gate, w_up, w_down):
    b, s, d = x.shape
    depth = w_ada.shape[0]
    cap = 2 * s // N_EXPERTS

    mod = _ada_call(c, w_ada, b_ada).reshape(depth, b, N_MOD, d)
    cos4, sin4 = _rope_tables(positions)
    w_in_p, w_uq_p, w_ukv_p = _permute_weights(w_in, w_uq, w_ukv)
    w_out_b = w_out.astype(BF16)
    wg_b = w_gate.astype(BF16)
    wu_b = w_up.astype(BF16)
    wd_b = w_down.astype(BF16)
    w_router_t = jnp.swapaxes(w_router, 1, 2)
    head = lax.bitcast_convert_type(lax.bitcast_convert_type(w_router_t, jnp.uint32) & jnp.uint32(0xFFFF0000), F32)
    w_router_split = jnp.concatenate([head.astype(BF16), (w_router_t - head).astype(BF16)], axis=1)

    for l in range(depth):
        gqn = q_head_g[l, :NOPE].reshape(1, NOPE)
        gqr2 = jnp.tile(q_head_g[l, NOPE:], 2).reshape(1, LANES)
        gkn = k_head_g[l, :NOPE].reshape(1, NOPE)
        gkr2 = jnp.tile(k_head_g[l, NOPE:], 2).reshape(1, LANES)
        q, k, v, y = _inproj_call(x, mod[l], norm1_g[l].reshape(1, d), w_in_p[l],
                                  q_latent_g[l].reshape(1, Q_RANK), w_uq_p[l],
                                  kv_latent_g[l].reshape(1, KV_RANK), w_ukv_p[l],
                                  gqn, gqr2, gkn, gkr2, cos4, sin4)
        attn = _attn_call(q, k, v)
        conv = _conv_call(y, conv_w[l], conv_b[l].reshape(1, CONV_CH),
                          conv_norm_g[l].reshape(1, CONV_CH), conv_norm_b[l].reshape(1, CONV_CH))
        x1, h2_tt, logits_t = _outproj_call(attn, conv, x, mod[l], w_out_b[l],
                                            norm2_g[l].reshape(1, d), w_router_split[l])
        idx, gate = _route_call(logits_t, cap)
        acc_tt = _moe_call(idx, gate, h2_tt, wg_b, wu_b, wd_b, l, d)
        x = _combine_call(x1, acc_tt, mod[l])
    return x
```

```python
import functools
import math

import jax
import jax.numpy as jnp
from jax import lax
from jax.experimental import pallas as pl
from jax.experimental.pallas import tpu as pltpu

F32 = jnp.float32
BF16 = jnp.bfloat16

D_MODEL = 1024
N_HEADS = 4
NOPE = 128
ROPE = 64
QK_DIM = NOPE + ROPE
V_DIM = 128
Q_RANK = 256
KV_RANK = 128
CONV_CH = 512
CONV_WIDTH = 31
CONV_PAD = CONV_WIDTH // 2
N_EXPERTS = 16
EXPERT_FF = 1024
N_MOD = 6
ROPE_THETA = 10000.0
EPS = 1e-6

LANES = 128
SUBLANES = 8
HALO = 16
MIB = 1024 * 1024


def _cparams(sem, vmem_mib):
    return pltpu.CompilerParams(dimension_semantics=sem, vmem_limit_bytes=vmem_mib * MIB)


def _ada_kernel(c_ref, w_ref, b_ref, o_ref):
    c = c_ref[...]
    ca = c * jax.nn.sigmoid(c)
    ca_hi = ca.astype(BF16)
    ca_lo = (ca - ca_hi.astype(F32)).astype(BF16)
    w = w_ref[0]
    w_hi = w.astype(BF16)
    w_lo = (w - w_hi.astype(F32)).astype(BF16)
    nb = c.shape[0]
    by_hi = jnp.dot(jnp.concatenate([ca_hi, ca_lo], axis=0), w_hi, preferred_element_type=F32)
    o_ref[0] = by_hi[:nb] + by_hi[nb:] + jnp.dot(ca_hi, w_lo, preferred_element_type=F32) + b_ref[0]


def _ada_call(c, w_ada, b_ada):
    depth, d, n = w_ada.shape
    b = c.shape[0]
    tn = 1536
    return pl.pallas_call(
        _ada_kernel,
        grid=(depth, n // tn),
        in_specs=[
            pl.BlockSpec((b, d), lambda l, j: (0, 0)),
            pl.BlockSpec((1, d, tn), lambda l, j: (l, 0, j)),
            pl.BlockSpec((1, 1, tn), lambda l, j: (l, 0, j)),
        ],
        out_specs=pl.BlockSpec((1, b, tn), lambda l, j: (l, 0, j)),
        out_shape=jax.ShapeDtypeStruct((depth, b, n), F32),
        compiler_params=_cparams(("parallel", "parallel"), 40),
        name="adaln_mod",
    )(c, w_ada, b_ada.reshape(depth, 1, n))


def _rope_kernel(pos_ref, invf_ref, cos_ref, sin_ref):
    half = ROPE // 2
    per_row = LANES // half
    ang = pos_ref[0].astype(F32) * invf_ref[...]
    packed_cos = jnp.cos(ang)
    packed_sin = jnp.sin(ang)
    n = ang.shape[0]
    group = lax.broadcasted_iota(jnp.int32, ang.shape, 1) // half
    sign = jnp.where(group % 2 == 0, -1.0, 1.0)

    def spread(p, j):
        out = p
        for q in range(per_row):
            if q != j:
                out = jnp.where(group == q, pltpu.roll(p, (half * (q - j)) % LANES, 1), out)
        return out

    for j in range(per_row):
        cos_ref[0, pl.ds(j, n, stride=per_row), :] = spread(packed_cos, j)
        sin_ref[0, pl.ds(j, n, stride=per_row), :] = spread(packed_sin, j) * sign


def _rope_tables(positions):
    b, s = positions.shape
    half = ROPE // 2
    per_row = LANES // half
    inv_freq = 1.0 / (ROPE_THETA ** (jnp.arange(0, ROPE, 2, dtype=F32) / ROPE))
    invf = jnp.tile(inv_freq, per_row).reshape(1, LANES)
    pos_rep = jnp.repeat(positions.reshape(b, s // per_row, per_row), half, axis=-1)
    return pl.pallas_call(
        _rope_kernel,
        grid=(b,),
        in_specs=[
            pl.BlockSpec((1, s // per_row, LANES), lambda i: (i, 0, 0)),
            pl.BlockSpec((1, LANES), lambda i: (0, 0)),
        ],
        out_specs=[pl.BlockSpec((1, s, LANES), lambda i: (i, 0, 0))] * 2,
        out_shape=[jax.ShapeDtypeStruct((b, s, LANES), F32)] * 2,
        compiler_params=_cparams(("parallel",), 32),
        name="rope_tables",
    )(pos_rep, invf)


def _inproj_kernel(x_ref, mod_ref, g1_ref, win_ref, qlg_ref, wuq_ref, kvlg_ref, wukv_ref,
                   gqn_ref, gqr_ref, gkn_ref, gkr_ref, cos_ref, sin_ref,
                   q_ref, k_ref, v_ref, y_ref, *, sub):
    for sb in range(x_ref.shape[1] // sub):
        _inproj_rows(slice(sb * sub, (sb + 1) * sub), x_ref, mod_ref, g1_ref, win_ref, qlg_ref, wuq_ref, kvlg_ref,
                     wukv_ref, gqn_ref, gqr_ref, gkn_ref, gkr_ref, cos_ref, sin_ref, q_ref, k_ref, v_ref, y_ref)


def _inproj_rows(rows, x_ref, mod_ref, g1_ref, win_ref, qlg_ref, wuq_ref, kvlg_ref, wukv_ref,
                 gqn_ref, gqr_ref, gkn_ref, gkr_ref, cos_ref, sin_ref, q_ref, k_ref, v_ref, y_ref):
    x = x_ref[0, rows, :]
    tm = x.shape[0]
    shift1 = mod_ref[0, 0:1, :]
    scale1 = mod_ref[0, 1:2, :]
    h = x * lax.rsqrt(jnp.mean(x * x, axis=-1, keepdims=True) + EPS) * g1_ref[...]
    h = h * (1.0 + scale1) + shift1
    proj = jnp.dot(h.astype(BF16), win_ref[...], preferred_element_type=F32)
    o1 = Q_RANK
    o2 = o1 + KV_RANK
    o3 = o2 + CONV_CH
    o4 = o3 + CONV_CH
    c_q = proj[:, :o1]
    c_kv = proj[:, o1:o2]
    y_ref[0, rows, :] = proj[:, o2:o3] * jax.nn.sigmoid(proj[:, o3:o4])
    kr = proj[:, o4:]

    cqn = c_q * lax.rsqrt(jnp.mean(c_q * c_q, axis=-1, keepdims=True) + EPS) * qlg_ref[...]
    qf = jnp.dot(cqn.astype(BF16), wuq_ref[...], preferred_element_type=F32)
    ckn = c_kv * lax.rsqrt(jnp.mean(c_kv * c_kv, axis=-1, keepdims=True) + EPS) * kvlg_ref[...]
    kvf = jnp.dot(ckn.astype(BF16), wukv_ref[...], preferred_element_type=F32)

    cos4 = cos_ref[0, rows, :]
    sin4 = sin_ref[0, rows, :]
    lane = lax.broadcasted_iota(jnp.int32, (tm, LANES), 1)
    first_half = (lane & (ROPE // 2)) == 0

    def rope_pair(xb):
        swapped = jnp.where(first_half, pltpu.roll(xb, LANES - ROPE // 2, 1), pltpu.roll(xb, ROPE // 2, 1))
        return xb * cos4 + swapped * sin4

    kr_rot = rope_pair(jnp.concatenate([kr, kr], axis=-1) * gkr_ref[...])[:, :ROPE]
    kr_ss = jnp.sum(kr * kr, axis=-1, keepdims=True)
    nn = N_HEADS * NOPE
    for hh in range(N_HEADS):
        kn = kvf[:, hh * NOPE:(hh + 1) * NOPE]
        ss = jnp.sum(kn * kn, axis=-1, keepdims=True) + kr_ss
        r = lax.rsqrt(ss / QK_DIM + EPS)
        k_ref[0, hh, rows, :NOPE] = (kn * r * gkn_ref[...]).astype(BF16)
        k_ref[0, hh, rows, NOPE:] = (kr_rot * r).astype(BF16)
        v_ref[0, hh, rows, :] = kvf[:, nn + hh * V_DIM:nn + (hh + 1) * V_DIM].astype(BF16)

    qscale = math.log2(math.e) / math.sqrt(QK_DIM)
    for pair in range(N_HEADS // 2):
        qr = qf[:, nn + pair * LANES:nn + (pair + 1) * LANES]
        qr_rot = rope_pair(qr * gqr_ref[...])
        qr_sq = qr * qr
        for j in range(2):
            hh = 2 * pair + j
            qn = qf[:, hh * NOPE:(hh + 1) * NOPE]
            ss = (jnp.sum(qn * qn, axis=-1, keepdims=True)
                  + jnp.sum(qr_sq[:, j * ROPE:(j + 1) * ROPE], axis=-1, keepdims=True))
            r = lax.rsqrt(ss / QK_DIM + EPS) * qscale
            q_ref[0, hh, rows, :NOPE] = (qn * r * gqn_ref[...]).astype(BF16)
            q_ref[0, hh, rows, NOPE:] = (qr_rot[:, j * ROPE:(j + 1) * ROPE] * r).astype(BF16)


def _inproj_call(x, mod_l, g1, w_in_p, qlg, w_uq_p, kvlg, w_ukv_p, gqn, gqr2, gkn, gkr2, cos4, sin4, tm=1024, sub=256):
    b, s, d = x.shape
    full = lambda shape: pl.BlockSpec(shape, lambda i, j: (0,) * len(shape))
    return pl.pallas_call(
        functools.partial(_inproj_kernel, sub=sub),
        grid=(b, s // tm),
        in_specs=[
            pl.BlockSpec((1, tm, d), lambda i, j: (i, j, 0)),
            pl.BlockSpec((1, N_MOD, d), lambda i, j: (i, 0, 0)),
            full((1, d)),
            full(w_in_p.shape),
            full((1, Q_RANK)),
            full(w_uq_p.shape),
            full((1, KV_RANK)),
            full(w_ukv_p.shape),
            full((1, NOPE)),
            full((1, LANES)),
            full((1, NOPE)),
            full((1, LANES)),
            pl.BlockSpec((1, tm, LANES), lambda i, j: (i, j, 0)),
            pl.BlockSpec((1, tm, LANES), lambda i, j: (i, j, 0)),
        ],
        out_specs=[
            pl.BlockSpec((1, N_HEADS, tm, QK_DIM), lambda i, j: (i, 0, j, 0)),
            pl.BlockSpec((1, N_HEADS, tm, QK_DIM), lambda i, j: (i, 0, j, 0)),
            pl.BlockSpec((1, N_HEADS, tm, V_DIM), lambda i, j: (i, 0, j, 0)),
            pl.BlockSpec((1, tm, CONV_CH), lambda i, j: (i, j, 0)),
        ],
        out_shape=[
            jax.ShapeDtypeStruct((b, N_HEADS, s, QK_DIM), BF16),
            jax.ShapeDtypeStruct((b, N_HEADS, s, QK_DIM), BF16),
            jax.ShapeDtypeStruct((b, N_HEADS, s, V_DIM), BF16),
            jax.ShapeDtypeStruct((b, s, CONV_CH), F32),
        ],
        compiler_params=_cparams(("parallel", "parallel"), 48),
        name="in_proj",
    )(x, mod_l, g1, w_in_p, qlg, w_uq_p, kvlg, w_ukv_p, gqn, gqr2, gkn, gkr2, cos4, sin4)


def _attn_kernel(q_ref, k_ref, v_ref, o_ref, *, sub):
    dv = v_ref.shape[-1]
    for hh in range(q_ref.shape[1]):
        k = k_ref[0, hh]
        v = v_ref[0, hh]
        for sb in range(q_ref.shape[2] // sub):
            rows = slice(sb * sub, (sb + 1) * sub)
            s2 = lax.dot_general(q_ref[0, hh, rows, :], k, (((1,), (1,)), ((), ())), preferred_element_type=F32)
            m = jnp.max(s2, axis=-1, keepdims=True)
            p = jnp.exp2(s2 - m)
            l = jnp.sum(p, axis=-1, keepdims=True)
            o = jnp.dot(p.astype(BF16), v, preferred_element_type=F32)
            o_ref[0, rows, hh * dv:(hh + 1) * dv] = (o / l).astype(BF16)


def _attn_call(q, k, v, hp=2, sub=512):
    b, nh, s, dk = q.shape
    dv = v.shape[-1]
    return pl.pallas_call(
        functools.partial(_attn_kernel, sub=sub),
        grid=(b, nh // hp),
        in_specs=[
            pl.BlockSpec((1, hp, s, dk), lambda i, h: (i, h, 0, 0)),
            pl.BlockSpec((1, hp, s, dk), lambda i, h: (i, h, 0, 0)),
            pl.BlockSpec((1, hp, s, dv), lambda i, h: (i, h, 0, 0)),
        ],
        out_specs=pl.BlockSpec((1, s, hp * dv), lambda i, h: (i, 0, h)),
        out_shape=jax.ShapeDtypeStruct((b, s, nh * dv), BF16),
        compiler_params=_cparams(("parallel", "parallel"), 48),
        name="attention",
    )(q, k, v)


def _conv_kernel(yc_ref, yp_ref, yn_ref, w_ref, cb_ref, lng_ref, lnb_ref, o_ref, win_ref, *, rc, rs):
    i = pl.program_id(1)
    nblk = pl.num_programs(1)
    win_ref[0, 0:HALO, :] = jnp.where(i > 0, yp_ref[0], 0.0)
    win_ref[0, HALO:HALO + rc, :] = yc_ref[0]
    win_ref[0, HALO + rc:HALO + rc + HALO, :] = jnp.where(i < nblk - 1, yn_ref[0], 0.0)
    span = rc + HALO + SUBLANES
    for sh in range(1, SUBLANES):
        win_ref[sh, 0:span, :] = win_ref[0, sh:sh + span, :]
    nch = CONV_CH // LANES
    for sub in range(rc // rs):
        r0 = sub * rs
        accs = [jnp.zeros((rs, LANES), F32) + cb_ref[:, c * LANES:(c + 1) * LANES] for c in range(nch)]
        for kk in range(CONV_WIDTH):
            off = HALO - CONV_PAD + kk
            base = r0 + off - off % SUBLANES
            for c in range(nch):
                cs = slice(c * LANES, (c + 1) * LANES)
                accs[c] = accs[c] + win_ref[off % SUBLANES, base:base + rs, cs] * w_ref[kk:kk + 1, cs]
        tot = accs[0]
        for c in range(1, nch):
            tot = tot + accs[c]
        mu = jnp.sum(tot, axis=-1, keepdims=True) / CONV_CH
        sq = jnp.zeros((rs, LANES), F32)
        for c in range(nch):
            dlt = accs[c] - mu
            sq = sq + dlt * dlt
        var = jnp.sum(sq, axis=-1, keepdims=True) / CONV_CH
        rstd = lax.rsqrt(var + EPS)
        for c in range(nch):
            cs = slice(c * LANES, (c + 1) * LANES)
            yn = (accs[c] - mu) * rstd * lng_ref[:, cs] + lnb_ref[:, cs]
            o_ref[0, r0:r0 + rs, cs] = (yn * jax.nn.sigmoid(yn)).astype(BF16)


def _conv_call(y, conv_w, conv_b, ln_g, ln_b, rc=512, rs=32):
    b, s, ch = y.shape
    hb = rc // HALO
    nh = s // HALO
    return pl.pallas_call(
        functools.partial(_conv_kernel, rc=rc, rs=rs),
        grid=(b, s // rc),
        in_specs=[
            pl.BlockSpec((1, rc, ch), lambda i, j: (i, j, 0)),
            pl.BlockSpec((1, HALO, ch), lambda i, j: (i, jnp.maximum(j * hb - 1, 0), 0)),
            pl.BlockSpec((1, HALO, ch), lambda i, j: (i, jnp.minimum((j + 1) * hb, nh - 1), 0)),
            pl.BlockSpec((CONV_WIDTH, ch), lambda i, j: (0, 0)),
            pl.BlockSpec((1, ch), lambda i, j: (0, 0)),
            pl.BlockSpec((1, ch), lambda i, j: (0, 0)),
            pl.BlockSpec((1, ch), lambda i, j: (0, 0)),
        ],
        out_specs=pl.BlockSpec((1, rc, ch), lambda i, j: (i, j, 0)),
        out_shape=jax.ShapeDtypeStruct((b, s, ch), BF16),
        scratch_shapes=[pltpu.VMEM((SUBLANES, rc + 2 * HALO, ch), F32)],
        compiler_params=_cparams(("parallel", "parallel"), 32),
        name="conv_group",
    )(y, y, y, conv_w, conv_b, ln_g, ln_b)


def _store_token_tiles(ref, lead, row0, val):
    r, d = val.shape
    nc = d // LANES
    for g in range(r // SUBLANES):
        for c in range(nc):
            dst = pl.ds((row0 + g * SUBLANES) * nc + c, SUBLANES, stride=nc)
            ref[lead + (dst, slice(None))] = val[g * SUBLANES:(g + 1) * SUBLANES, c * LANES:(c + 1) * LANES]


def _load_token_tiles(ref, lead, row0, r, d):
    nc = d // LANES
    groups = []
    for g in range(r // SUBLANES):
        groups.append(jnp.concatenate(
            [ref[lead + (pl.ds((row0 + g * SUBLANES) * nc + c, SUBLANES, stride=nc), slice(None))]
             for c in range(nc)], axis=-1))
    return jnp.concatenate(groups, axis=0)


def _outproj_kernel(a_ref, cv_ref, x_ref, mod_ref, wo_ref, g2_ref, wrt_ref, x1_ref, h2_ref, lg_ref, *, sub):
    half = a_ref.shape[-1]
    ne = lg_ref.shape[1]
    gate1 = mod_ref[0, 2:3, :]
    shift2 = mod_ref[0, 3:4, :]
    scale2 = mod_ref[0, 4:5, :]
    for sb in range(x_ref.shape[1] // sub):
        rows = slice(sb * sub, (sb + 1) * sub)
        mix = (jnp.dot(a_ref[0, rows, :], wo_ref[:half, :], preferred_element_type=F32)
               + jnp.dot(cv_ref[0, rows, :], wo_ref[half:, :], preferred_element_type=F32))
        x1 = x_ref[0, rows, :] + gate1 * mix
        x1_ref[0, rows, :] = x1
        h2 = x1 * lax.rsqrt(jnp.mean(x1 * x1, axis=-1, keepdims=True) + EPS) * g2_ref[...]
        h2 = h2 * (1.0 + scale2) + shift2
        _store_token_tiles(h2_ref, (0,), sb * sub, h2)
        h2_hi = h2.astype(BF16)
        h2_lo = (h2 - h2_hi.astype(F32)).astype(BF16)
        nt = (((1,), (1,)), ((), ()))
        by_hi = lax.dot_general(wrt_ref[...], h2_hi, nt, preferred_element_type=F32)
        by_lo = lax.dot_general(wrt_ref[:ne, :], h2_lo, nt, preferred_element_type=F32)
        lg_ref[0, :, rows] = by_hi[:ne] + by_hi[ne:] + by_lo


def _outproj_call(attn, conv, x, mod_l, w_out_b, g2, w_router_split, tm=1024, sub=256):
    b, s, d = x.shape
    half = attn.shape[-1]
    ne = w_router_split.shape[0] // 2
    return pl.pallas_call(
        functools.partial(_outproj_kernel, sub=sub),
        grid=(b, s // tm),
        in_specs=[
            pl.BlockSpec((1, tm, half), lambda i, j: (i, j, 0)),
            pl.BlockSpec((1, tm, half), lambda i, j: (i, j, 0)),
            pl.BlockSpec((1, tm, d), lambda i, j: (i, j, 0)),
            pl.BlockSpec((1, N_MOD, d), lambda i, j: (i, 0, 0)),
            pl.BlockSpec((2 * half, d), lambda i, j: (0, 0)),
            pl.BlockSpec((1, d), lambda i, j: (0, 0)),
            pl.BlockSpec((2 * ne, d), lambda i, j: (0, 0)),
        ],
        out_specs=[
            pl.BlockSpec((1, tm, d), lambda i, j: (i, j, 0)),
            pl.BlockSpec((1, tm * d // LANES, LANES), lambda i, j: (i, j, 0)),
            pl.BlockSpec((1, ne, tm), lambda i, j: (i, 0, j)),
        ],
        out_shape=[
            jax.ShapeDtypeStruct((b, s, d), F32),
            jax.ShapeDtypeStruct((b, s * d // LANES, LANES), F32),
            jax.ShapeDtypeStruct((b, ne, s), F32),
        ],
        compiler_params=_cparams(("parallel", "parallel"), 48),
        name="out_proj",
    )(attn, conv, x, mod_l, w_out_b, g2, w_router_split)


def _prefix_lanes(xb):
    r, s = xb.shape
    row = lax.broadcasted_iota(jnp.int32, (LANES, LANES), 0)
    col = lax.broadcasted_iota(jnp.int32, (LANES, LANES), 1)
    tri = jnp.where(row <= col, 1.0, 0.0).astype(BF16)
    carry = jnp.zeros((r, 1), F32)
    outs = []
    for blk in range(s // LANES):
        pb = jnp.dot(xb[:, blk * LANES:(blk + 1) * LANES], tri, preferred_element_type=F32) + carry
        outs.append(pb)
        carry = pb[:, LANES - 1:LANES]
    return jnp.concatenate(outs, axis=-1)


def _route_kernel(lg_ref, idx_ref, gate_ref, *, cap, refine):
    lg = lg_ref[0]
    ne, s = lg.shape
    idx_bits = (s - 1).bit_length()
    m = jnp.max(lg, axis=0, keepdims=True)
    ex = jnp.exp(lg - m)
    aff = ex / jnp.sum(ex, axis=0, keepdims=True)

    def count_ge(thr):
        return jnp.sum(jnp.where(aff >= thr, 1.0, 0.0), axis=-1, keepdims=True)

    def enough(bits):
        return count_ge(pltpu.bitcast(bits, F32)) >= cap

    cur = jnp.zeros((ne, 1), jnp.int32)
    for bit in range(30, 0, -2):
        c_hi = cur | (1 << bit)
        c_lo = cur | (1 << (bit - 1))
        c_both = c_hi | (1 << (bit - 1))
        cur = jnp.where(enough(c_both), c_both, jnp.where(enough(c_hi), c_hi, jnp.where(enough(c_lo), c_lo, cur)))
    cur = jnp.where(enough(cur | 1), cur | 1, cur)
    lo = pltpu.bitcast(cur, F32)
    hi = pltpu.bitcast(cur + 1, F32)
    for _ in range(refine):
        t1 = lo + 0.25 * (hi - lo)
        t2 = lo + 0.5 * (hi - lo)
        t3 = lo + 0.75 * (hi - lo)
        e1 = count_ge(t1) >= cap
        e2 = count_ge(t2) >= cap
        e3 = count_ge(t3) >= cap
        lo, hi = (jnp.where(e3, t3, jnp.where(e2, t2, jnp.where(e1, t1, lo))),
                  jnp.where(e1, jnp.where(e2, jnp.where(e3, hi, t3), t2), t1))
    above = aff >= hi
    window = (aff >= lo) & jnp.logical_not(above)
    c_above = _prefix_lanes(jnp.where(above, 1.0, 0.0).astype(BF16))
    c_win = _prefix_lanes(jnp.where(window, 1.0, 0.0).astype(BF16))
    need = cap - c_above[:, -1:]
    sel = above | (window & (c_win <= need))
    rank = (c_above + jnp.minimum(c_win, need)).astype(jnp.int32) - 1
    t = lax.broadcasted_iota(jnp.int32, (ne, s), 1)
    w = jnp.where(sel, ((t - rank) << (idx_bits + 1)) | (t << 1) | 1, 0)
    g = aff
    for kk in range(idx_bits):
        step = 1 << kk
        w_in = pltpu.roll(w, s - step, 1)
        g_in = pltpu.roll(g, s - step, 1)
        take = ((w_in >> (idx_bits + 1 + kk)) & w_in & 1) == 1
        stay = ((w >> (idx_bits + 1 + kk)) & 1) == 0
        w = jnp.where(take, w_in, jnp.where(stay, w, 0))
        g = jnp.where(take, g_in, g)
    idx_ref[0] = (w[:, :cap] >> 1) & ((1 << idx_bits) - 1)
    gate_ref[0] = g[:, :cap]


def _route_call(logits_t, cap, refine=3):
    b, ne, s = logits_t.shape
    return pl.pallas_call(
        functools.partial(_route_kernel, cap=cap, refine=refine),
        grid=(b,),
        in_specs=[pl.BlockSpec((1, ne, s), lambda i: (i, 0, 0))],
        out_specs=[pl.BlockSpec((1, ne, cap), lambda i: (i, 0, 0))] * 2,
        out_shape=[jax.ShapeDtypeStruct((b, ne, cap), jnp.int32), jax.ShapeDtypeStruct((b, ne, cap), F32)],
        compiler_params=_cparams(("parallel",), 32),
        name="route",
    )(logits_t)


def _moe_kernel(idx_ref, gate_ref, h2_hbm, wg_ref, wu_ref, wd_ref, acc_ref, xe_ref, ye_ref, h2_ref, h2_sem,
                *, cap, grp, d):
    seq = pl.program_id(0)
    e = pl.program_id(1)
    last = pl.num_programs(1) - 1
    hs = seq % 2

    def h2_copy(which, into):
        return pltpu.make_async_copy(h2_hbm.at[which], h2_ref.at[into], h2_sem.at[into])

    @pl.when((seq == 0) & (e == 0))
    def _():
        h2_copy(0, 0).start()

    @pl.when(e == 0)
    def _():
        h2_copy(seq, hs).wait()

    @pl.when((e == 1) & (seq + 1 < pl.num_programs(0)))
    def _():
        h2_copy(seq + 1, 1 - hs).start()

    nc = d // LANES
    slot = e % 2
    other = 1 - slot
    this_base = e * cap
    prev_base = jnp.maximum(e - 1, 0) * cap
    next_base = jnp.minimum(e + 1, last) * cap

    def token_rows(tok):
        return pl.ds(pl.multiple_of(tok * nc, nc), nc)

    def accumulate(base, y_slot, i):
        toks = [idx_ref[0, 0, base + i * grp + u] for u in range(grp)]
        new = [acc_ref[0, token_rows(toks[u]), :]
               + gate_ref[0, 0, base + i * grp + u] * ye_ref[y_slot, token_rows(i * grp + u), :] for u in range(grp)]
        for u in range(grp):
            acc_ref[0, token_rows(toks[u]), :] = new[u]

    @pl.when(e == 0)
    def _():
        acc_ref[...] = jnp.zeros(acc_ref.shape, F32)
        ye_ref[1] = jnp.zeros(ye_ref.shape[1:], F32)

        def gather(i, carry):
            for u in range(grp):
                j = i * grp + u
                xe_ref[0, token_rows(j), :] = h2_ref[hs, token_rows(idx_ref[0, 0, j]), :]
            return carry

        lax.fori_loop(0, cap // grp, gather, 0)

    xe = _load_token_tiles(xe_ref, (slot,), 0, cap, d).astype(BF16)
    for i in range(cap // grp):
        accumulate(prev_base, other, i)
    for j in range(cap):
        xe_ref[other, token_rows(j), :] = h2_ref[hs, token_rows(idx_ref[0, 0, next_base + j]), :]
    g = jnp.dot(xe, wg_ref[0, 0], preferred_element_type=F32)
    hid = (g * jax.nn.sigmoid(g)) * jnp.dot(xe, wu_ref[0, 0], preferred_element_type=F32)
    _store_token_tiles(ye_ref, (slot,), 0, jnp.dot(hid.astype(BF16), wd_ref[0, 0], preferred_element_type=F32))

    @pl.when(e == last)
    def _():
        def tail(i, carry):
            accumulate(this_base, slot, i)
            return carry

        lax.fori_loop(0, cap // grp, tail, 0)


def _moe_call(idx, gate, h2_tt, wg_b, wu_b, wd_b, layer, d, grp=8):
    b, rows, _ = h2_tt.shape
    ne, cap = idx.shape[1:]
    ff = wd_b.shape[2]
    nc = d // LANES
    per_sequence = pl.BlockSpec((1, 1, ne * cap), lambda i, e: (i, 0, 0), memory_space=pltpu.SMEM)
    return pl.pallas_call(
        functools.partial(_moe_kernel, cap=cap, grp=grp, d=d),
        grid=(b, ne),
        in_specs=[
            per_sequence, per_sequence,
            pl.BlockSpec(memory_space=pl.ANY),
            pl.BlockSpec((1, 1, d, ff), lambda i, e: (layer, e, 0, 0)),
            pl.BlockSpec((1, 1, d, ff), lambda i, e: (layer, e, 0, 0)),
            pl.BlockSpec((1, 1, ff, d), lambda i, e: (layer, e, 0, 0)),
        ],
        out_specs=pl.BlockSpec((1, rows, LANES), lambda i, e: (i, 0, 0)),
        out_shape=jax.ShapeDtypeStruct((b, rows, LANES), F32),
        scratch_shapes=[pltpu.VMEM((2, cap * nc, LANES), F32), pltpu.VMEM((2, cap * nc, LANES), F32),
                        pltpu.VMEM((2, rows, LANES), F32), pltpu.SemaphoreType.DMA((2,))],
        compiler_params=_cparams(("arbitrary", "arbitrary"), 56),
        name="moe_experts",
    )(idx.reshape(b, 1, ne * cap), gate.reshape(b, 1, ne * cap), h2_tt, wg_b, wu_b, wd_b)


def _combine_kernel(x1_ref, acc_ref, mod_ref, o_ref, *, sub):
    gate2 = mod_ref[0, 5:6, :]
    d = x1_ref.shape[-1]
    for sb in range(x1_ref.shape[1] // sub):
        rows = slice(sb * sub, (sb + 1) * sub)
        o_ref[0, rows, :] = x1_ref[0, rows, :] + gate2 * _load_token_tiles(acc_ref, (0,), sb * sub, sub, d)


def _combine_call(x1, acc_tt, mod_l, tm=512, sub=128):
    b, s, d = x1.shape
    return pl.pallas_call(
        functools.partial(_combine_kernel, sub=sub),
        grid=(b, s // tm),
        in_specs=[
            pl.BlockSpec((1, tm, d), lambda i, j: (i, j, 0)),
            pl.BlockSpec((1, tm * d // LANES, LANES), lambda i, j: (i, j, 0)),
            pl.BlockSpec((1, N_MOD, d), lambda i, j: (i, 0, 0)),
        ],
        out_specs=pl.BlockSpec((1, tm, d), lambda i, j: (i, j, 0)),
        out_shape=jax.ShapeDtypeStruct((b, s, d), F32),
        compiler_params=_cparams(("parallel", "parallel"), 32),
        name="moe_combine",
    )(x1, acc_tt, mod_l)


def _permute_weights(w_in, w_uq, w_ukv):
    o1 = Q_RANK
    o2 = o1 + KV_RANK
    o3 = o2 + ROPE
    w_in_p = jnp.concatenate([w_in[..., :o2], w_in[..., o3:], w_in[..., o2:o3]], axis=-1)
    depth, qr, _ = w_uq.shape
    wq = w_uq.reshape(depth, qr, N_HEADS, QK_DIM)
    w_uq_p = jnp.concatenate([wq[..., :NOPE].reshape(depth, qr, N_HEADS * NOPE),
                              wq[..., NOPE:].reshape(depth, qr, N_HEADS * ROPE)], axis=-1)
    kr = w_ukv.shape[1]
    wkv = w_ukv.reshape(depth, kr, N_HEADS, NOPE + V_DIM)
    w_ukv_p = jnp.concatenate([wkv[..., :NOPE].reshape(depth, kr, N_HEADS * NOPE),
                               wkv[..., NOPE:].reshape(depth, kr, N_HEADS * V_DIM)], axis=-1)
    return w_in_p.astype(BF16), w_uq_p.astype(BF16), w_ukv_p.astype(BF16)


def kernel(x, c, positions, norm1_g, w_ada, b_ada, w_in, q_latent_g, w_uq, kv_latent_g, w_ukv, q_head_g, k_head_g, conv_w, conv_b, conv_norm_g, conv_norm_b, w_out, norm2_g, w_router, w_gate, w_up, w_down):
    b, s, d = x.shape
    depth = w_ada.shape[0]
    cap = 2 * s // N_EXPERTS

    mod = _ada_call(c, w_ada, b_ada).reshape(depth, b, N_MOD, d)
    cos4, sin4 = _rope_tables(positions)
    w_in_p, w_uq_p, w_ukv_p = _permute_weights(w_in, w_uq, w_ukv)
    w_out_b = w_out.astype(BF16)
    wg_b = w_gate.astype(BF16)
    wu_b = w_up.astype(BF16)
    wd_b = w_down.astype(BF16)
    w_router_t = jnp.swapaxes(w_router, 1, 2)
    head = lax.bitcast_convert_type(lax.bitcast_convert_type(w_router_t, jnp.uint32) & jnp.uint32(0xFFFF0000), F32)
    w_router_split = jnp.concatenate([head.astype(BF16), (w_router_t - head).astype(BF16)], axis=1)

    for l in range(depth):
        gqn = q_head_g[l, :NOPE].reshape(1, NOPE)
        gqr2 = jnp.tile(q_head_g[l, NOPE:], 2).reshape(1, LANES)
        gkn = k_head_g[l, :NOPE].reshape(1, NOPE)
        gkr2 = jnp.tile(k_head_g[l, NOPE:], 2).reshape(1, LANES)
        q, k, v, y = _inproj_call(x, mod[l], norm1_g[l].reshape(1, d), w_in_p[l],
                                  q_latent_g[l].reshape(1, Q_RANK), w_uq_p[l],
                                  kv_latent_g[l].reshape(1, KV_RANK), w_ukv_p[l],
                                  gqn, gqr2, gkn, gkr2, cos4, sin4)
        attn = _attn_call(q, k, v)
        conv = _conv_call(y, conv_w[l], conv_b[l].reshape(1, CONV_CH),
                          conv_norm_g[l].reshape(1, CONV_CH), conv_norm_b[l].reshape(1, CONV_CH))
        x1, h2_tt, logits_t = _outproj_call(attn, conv, x, mod[l], w_out_b[l],
                                            norm2_g[l].reshape(1, d), w_router_split[l])
        idx, gate = _route_call(logits_t, cap)
        acc_tt = _moe_call(idx, gate, h2_tt, wg_b, wu_b, wd_b, l, d)
        x = _combine_call(x1, acc_tt, mod[l])
    return x
```

```python
import functools
import math

import jax
import jax.numpy as jnp
from jax import lax
from jax.experimental import pallas as pl
from jax.experimental.pallas import tpu as pltpu

F32 = jnp.float32
BF16 = jnp.bfloat16

D_MODEL = 1024
N_HEADS = 4
NOPE = 128
ROPE = 64
QK_DIM = NOPE + ROPE
V_DIM = 128
Q_RANK = 256
KV_RANK = 128
CONV_CH = 512
CONV_WIDTH = 31
CONV_PAD = CONV_WIDTH // 2
N_EXPERTS = 16
EXPERT_FF = 1024
N_MOD = 6
ROPE_THETA = 10000.0
EPS = 1e-6

LANES = 128
SUBLANES = 8
HALO = 16
MIB = 1024 * 1024


def _cparams(sem, vmem_mib):
    return pltpu.CompilerParams(dimension_semantics=sem, vmem_limit_bytes=vmem_mib * MIB)


def _ada_kernel(c_ref, w_ref, b_ref, o_ref):
    c = c_ref[...]
    ca = c * jax.nn.sigmoid(c)
    ca_hi = ca.astype(BF16)
    ca_lo = (ca - ca_hi.astype(F32)).astype(BF16)
    w = w_ref[0]
    w_hi = w.astype(BF16)
    w_lo = (w - w_hi.astype(F32)).astype(BF16)
    nb = c.shape[0]
    by_hi = jnp.dot(jnp.concatenate([ca_hi, ca_lo], axis=0), w_hi, preferred_element_type=F32)
    o_ref[0] = by_hi[:nb] + by_hi[nb:] + jnp.dot(ca_hi, w_lo, preferred_element_type=F32) + b_ref[0]


def _ada_call(c, w_ada, b_ada):
    depth, d, n = w_ada.shape
    b = c.shape[0]
    tn = 1536
    return pl.pallas_call(
        _ada_kernel,
        grid=(depth, n // tn),
        in_specs=[
            pl.BlockSpec((b, d), lambda l, j: (0, 0)),
            pl.BlockSpec((1, d, tn), lambda l, j: (l, 0, j)),
            pl.BlockSpec((1, 1, tn), lambda l, j: (l, 0, j)),
        ],
        out_specs=pl.BlockSpec((1, b, tn), lambda l, j: (l, 0, j)),
        out_shape=jax.ShapeDtypeStruct((depth, b, n), F32),
        compiler_params=_cparams(("parallel", "parallel"), 40),
        name="adaln_mod",
    )(c, w_ada, b_ada.reshape(depth, 1, n))


def _rope_kernel(pos_ref, invf_ref, cos_ref, sin_ref):
    half = ROPE // 2
    per_row = LANES // half
    ang = pos_ref[0].astype(F32) * invf_ref[...]
    packed_cos = jnp.cos(ang)
    packed_sin = jnp.sin(ang)
    n = ang.shape[0]
    group = lax.broadcasted_iota(jnp.int32, ang.shape, 1) // half
    sign = jnp.where(group % 2 == 0, -1.0, 1.0)

    def spread(p, j):
        out = p
        for q in range(per_row):
            if q != j:
                out = jnp.where(group == q, pltpu.roll(p, (half * (q - j)) % LANES, 1), out)
        return out

    for j in range(per_row):
        cos_ref[0, pl.ds(j, n, stride=per_row), :] = spread(packed_cos, j)
        sin_ref[0, pl.ds(j, n, stride=per_row), :] = spread(packed_sin, j) * sign


def _rope_tables(positions):
    b, s = positions.shape
    half = ROPE // 2
    per_row = LANES // half
    inv_freq = 1.0 / (ROPE_THETA ** (jnp.arange(0, ROPE, 2, dtype=F32) / ROPE))
    invf = jnp.tile(inv_freq, per_row).reshape(1, LANES)
    pos_rep = jnp.repeat(positions.reshape(b, s // per_row, per_row), half, axis=-1)
    return pl.pallas_call(
        _rope_kernel,
        grid=(b,),
        in_specs=[
            pl.BlockSpec((1, s // per_row, LANES), lambda i: (i, 0, 0)),
            pl.BlockSpec((1, LANES), lambda i: (0, 0)),
        ],
        out_specs=[pl.BlockSpec((1, s, LANES), lambda i: (i, 0, 0))] * 2,
        out_shape=[jax.ShapeDtypeStruct((b, s, LANES), F32)] * 2,
        compiler_params=_cparams(("parallel",), 32),
        name="rope_tables",
    )(pos_rep, invf)


def _inproj_kernel(x_ref, mod_ref, g1_ref, win_ref, qlg_ref, wuq_ref, kvlg_ref, wukv_ref,
                   gqn_ref, gqr_ref, gkn_ref, gkr_ref, cos_ref, sin_ref,
                   q_ref, k_ref, v_ref, y_ref, *, sub):
    for sb in range(x_ref.shape[1] // sub):
        _inproj_rows(slice(sb * sub, (sb + 1) * sub), x_ref, mod_ref, g1_ref, win_ref, qlg_ref, wuq_ref, kvlg_ref,
                     wukv_ref, gqn_ref, gqr_ref, gkn_ref, gkr_ref, cos_ref, sin_ref, q_ref, k_ref, v_ref, y_ref)


def _inproj_rows(rows, x_ref, mod_ref, g1_ref, win_ref, qlg_ref, wuq_ref, kvlg_ref, wukv_ref,
                 gqn_ref, gqr_ref, gkn_ref, gkr_ref, cos_ref, sin_ref, q_ref, k_ref, v_ref, y_ref):
    x = x_ref[0, rows, :]
    tm = x.shape[0]
    shift1 = mod_ref[0, 0:1, :]
    scale1 = mod_ref[0, 1:2, :]
    h = x * lax.rsqrt(jnp.mean(x * x, axis=-1, keepdims=True) + EPS) * g1_ref[...]
    h = h * (1.0 + scale1) + shift1
    proj = jnp.dot(h.astype(BF16), win_ref[...], preferred_element_type=F32)
    o1 = Q_RANK
    o2 = o1 + KV_RANK
    o3 = o2 + CONV_CH
    o4 = o3 + CONV_CH
    c_q = proj[:, :o1]
    c_kv = proj[:, o1:o2]
    y_ref[0, rows, :] = proj[:, o2:o3] * jax.nn.sigmoid(proj[:, o3:o4])
    kr = proj[:, o4:]

    cqn = c_q * lax.rsqrt(jnp.mean(c_q * c_q, axis=-1, keepdims=True) + EPS) * qlg_ref[...]
    qf = jnp.dot(cqn.astype(BF16), wuq_ref[...], preferred_element_type=F32)
    ckn = c_kv * lax.rsqrt(jnp.mean(c_kv * c_kv, axis=-1, keepdims=True) + EPS) * kvlg_ref[...]
    kvf = jnp.dot(ckn.astype(BF16), wukv_ref[...], preferred_element_type=F32)

    cos4 = cos_ref[0, rows, :]
    sin4 = sin_ref[0, rows, :]
    lane = lax.broadcasted_iota(jnp.int32, (tm, LANES), 1)
    first_half = (lane & (ROPE // 2)) == 0

    def rope_pair(xb):
        swapped = jnp.where(first_half, pltpu.roll(xb, LANES - ROPE // 2, 1), pltpu.roll(xb, ROPE // 2, 1))
        return xb * cos4 + swapped * sin4

    kr_rot = rope_pair(jnp.concatenate([kr, kr], axis=-1) * gkr_ref[...])[:, :ROPE]
    kr_ss = jnp.sum(kr * kr, axis=-1, keepdims=True)
    nn = N_HEADS * NOPE
    for hh in range(N_HEADS):
        kn = kvf[:, hh * NOPE:(hh + 1) * NOPE]
        ss = jnp.sum(kn * kn, axis=-1, keepdims=True) + kr_ss
        r = lax.rsqrt(ss / QK_DIM + EPS)
        k_ref[0, hh, rows, :NOPE] = (kn * r * gkn_ref[...]).astype(BF16)
        k_ref[0, hh, rows, NOPE:] = (kr_rot * r).astype(BF16)
        v_ref[0, hh, rows, :] = kvf[:, nn + hh * V_DIM:nn + (hh + 1) * V_DIM].astype(BF16)

    qscale = math.log2(math.e) / math.sqrt(QK_DIM)
    for pair in range(N_HEADS // 2):
        qr = qf[:, nn + pair * LANES:nn + (pair + 1) * LANES]
        qr_rot = rope_pair(qr * gqr_ref[...])
        qr_sq = qr * qr
        for j in range(2):
            hh = 2 * pair + j
            qn = qf[:, hh * NOPE:(hh + 1) * NOPE]
            ss = (jnp.sum(qn * qn, axis=-1, keepdims=True)
                  + jnp.sum(qr_sq[:, j * ROPE:(j + 1) * ROPE], axis=-1, keepdims=True))
            r = lax.rsqrt(ss / QK_DIM + EPS) * qscale
            q_ref[0, hh, rows, :NOPE] = (qn * r * gqn_ref[...]).astype(BF16)
            q_ref[0, hh, rows, NOPE:] = (qr_rot[:, j * ROPE:(j + 1) * ROPE] * r).astype(BF16)


def _inproj_call(x, mod_l, g1, w_in_p, qlg, w_uq_p, kvlg, w_ukv_p, gqn, gqr2, gkn, gkr2, cos4, sin4, tm=1024, sub=256):
    b, s, d = x.shape
    full = lambda shape: pl.BlockSpec(shape, lambda i, j: (0,) * len(shape))
    return pl.pallas_call(
        functools.partial(_inproj_kernel, sub=sub),
        grid=(b, s // tm),
        in_specs=[
            pl.BlockSpec((1, tm, d), lambda i, j: (i, j, 0)),
            pl.BlockSpec((1, N_MOD, d), lambda i, j: (i, 0, 0)),
            full((1, d)),
            full(w_in_p.shape),
            full((1, Q_RANK)),
            full(w_uq_p.shape),
            full((1, KV_RANK)),
            full(w_ukv_p.shape),
            full((1, NOPE)),
            full((1, LANES)),
            full((1, NOPE)),
            full((1, LANES)),
            pl.BlockSpec((1, tm, LANES), lambda i, j: (i, j, 0)),
            pl.BlockSpec((1, tm, LANES), lambda i, j: (i, j, 0)),
        ],
        out_specs=[
            pl.BlockSpec((1, N_HEADS, tm, QK_DIM), lambda i, j: (i, 0, j, 0)),
            pl.BlockSpec((1, N_HEADS, tm, QK_DIM), lambda i, j: (i, 0, j, 0)),
            pl.BlockSpec((1, N_HEADS, tm, V_DIM), lambda i, j: (i, 0, j, 0)),
            pl.BlockSpec((1, tm, CONV_CH), lambda i, j: (i, j, 0)),
        ],
        out_shape=[
            jax.ShapeDtypeStruct((b, N_HEADS, s, QK_DIM), BF16),
            jax.ShapeDtypeStruct((b, N_HEADS, s, QK_DIM), BF16),
            jax.ShapeDtypeStruct((b, N_HEADS, s, V_DIM), BF16),
            jax.ShapeDtypeStruct((b, s, CONV_CH), F32),
        ],
        compiler_params=_cparams(("parallel", "parallel"), 48),
        name="in_proj",
    )(x, mod_l, g1, w_in_p, qlg, w_uq_p, kvlg, w_ukv_p, gqn, gqr2, gkn, gkr2, cos4, sin4)


def _attn_kernel(q_ref, k_ref, v_ref, o_ref, *, sub):
    dv = v_ref.shape[-1]
    for hh in range(q_ref.shape[1]):
        k = k_ref[0, hh]
        v = v_ref[0, hh]
        sizes = [sub // 2] + [sub] * (q_ref.shape[2] // sub - 1) + [sub // 2]
        for bi, rn in enumerate(sizes):
            rows = slice(sum(sizes[:bi]), sum(sizes[:bi]) + rn)
            s2 = lax.dot_general(q_ref[0, hh, rows, :], k, (((1,), (1,)), ((), ())), preferred_element_type=F32)
            m = jnp.max(s2, axis=-1, keepdims=True)
            p = jnp.exp2(s2 - m)
            l = jnp.sum(p, axis=-1, keepdims=True)
            o = jnp.dot(p.astype(BF16), v, preferred_element_type=F32)
            o_ref[0, rows, hh * dv:(hh + 1) * dv] = (o / l).astype(BF16)


def _attn_call(q, k, v, hp=2, sub=512):
    b, nh, s, dk = q.shape
    dv = v.shape[-1]
    return pl.pallas_call(
        functools.partial(_attn_kernel, sub=sub),
        grid=(b, nh // hp),
        in_specs=[
            pl.BlockSpec((1, hp, s, dk), lambda i, h: (i, h, 0, 0)),
            pl.BlockSpec((1, hp, s, dk), lambda i, h: (i, h, 0, 0)),
            pl.BlockSpec((1, hp, s, dv), lambda i, h: (i, h, 0, 0)),
        ],
        out_specs=pl.BlockSpec((1, s, hp * dv), lambda i, h: (i, 0, h)),
        out_shape=jax.ShapeDtypeStruct((b, s, nh * dv), BF16),
        compiler_params=_cparams(("parallel", "parallel"), 48),
        name="attention",
    )(q, k, v)


def _conv_kernel(yc_ref, yp_ref, yn_ref, w_ref, cb_ref, lng_ref, lnb_ref, o_ref, win_ref, *, rc, rs):
    i = pl.program_id(1)
    nblk = pl.num_programs(1)
    win_ref[0, 0:HALO, :] = jnp.where(i > 0, yp_ref[0], 0.0)
    win_ref[0, HALO:HALO + rc, :] = yc_ref[0]
    win_ref[0, HALO + rc:HALO + rc + HALO, :] = jnp.where(i < nblk - 1, yn_ref[0], 0.0)
    span = rc + HALO + SUBLANES
    for sh in range(1, SUBLANES):
        win_ref[sh, 0:span, :] = win_ref[0, sh:sh + span, :]
    nch = CONV_CH // LANES
    for sub in range(rc // rs):
        r0 = sub * rs
        accs = [jnp.zeros((rs, LANES), F32) + cb_ref[:, c * LANES:(c + 1) * LANES] for c in range(nch)]
        for kk in range(CONV_WIDTH):
            off = HALO - CONV_PAD + kk
            base = r0 + off - off % SUBLANES
            for c in range(nch):
                cs = slice(c * LANES, (c + 1) * LANES)
                accs[c] = accs[c] + win_ref[off % SUBLANES, base:base + rs, cs] * w_ref[kk:kk + 1, cs]
        tot = accs[0]
        for c in range(1, nch):
            tot = tot + accs[c]
        mu = jnp.sum(tot, axis=-1, keepdims=True) / CONV_CH
        sq = jnp.zeros((rs, LANES), F32)
        for c in range(nch):
            dlt = accs[c] - mu
            sq = sq + dlt * dlt
        var = jnp.sum(sq, axis=-1, keepdims=True) / CONV_CH
        rstd = lax.rsqrt(var + EPS)
        for c in range(nch):
            cs = slice(c * LANES, (c + 1) * LANES)
            yn = (accs[c] - mu) * rstd * lng_ref[:, cs] + lnb_ref[:, cs]
            o_ref[0, r0:r0 + rs, cs] = (yn * jax.nn.sigmoid(yn)).astype(BF16)


def _conv_call(y, conv_w, conv_b, ln_g, ln_b, rc=512, rs=32):
    b, s, ch = y.shape
    hb = rc // HALO
    nh = s // HALO
    return pl.pallas_call(
        functools.partial(_conv_kernel, rc=rc, rs=rs),
        grid=(b, s // rc),
        in_specs=[
            pl.BlockSpec((1, rc, ch), lambda i, j: (i, j, 0)),
            pl.BlockSpec((1, HALO, ch), lambda i, j: (i, jnp.maximum(j * hb - 1, 0), 0)),
            pl.BlockSpec((1, HALO, ch), lambda i, j: (i, jnp.minimum((j + 1) * hb, nh - 1), 0)),
            pl.BlockSpec((CONV_WIDTH, ch), lambda i, j: (0, 0)),
            pl.BlockSpec((1, ch), lambda i, j: (0, 0)),
            pl.BlockSpec((1, ch), lambda i, j: (0, 0)),
            pl.BlockSpec((1, ch), lambda i, j: (0, 0)),
        ],
        out_specs=pl.BlockSpec((1, rc, ch), lambda i, j: (i, j, 0)),
        out_shape=jax.ShapeDtypeStruct((b, s, ch), BF16),
        scratch_shapes=[pltpu.VMEM((SUBLANES, rc + 2 * HALO, ch), F32)],
        compiler_params=_cparams(("parallel", "parallel"), 32),
        name="conv_group",
    )(y, y, y, conv_w, conv_b, ln_g, ln_b)


def _store_token_tiles(ref, lead, row0, val):
    r, d = val.shape
    nc = d // LANES
    for g in range(r // SUBLANES):
        for c in range(nc):
            dst = pl.ds((row0 + g * SUBLANES) * nc + c, SUBLANES, stride=nc)
            ref[lead + (dst, slice(None))] = val[g * SUBLANES:(g + 1) * SUBLANES, c * LANES:(c + 1) * LANES]


def _load_token_tiles(ref, lead, row0, r, d):
    nc = d // LANES
    groups = []
    for g in range(r // SUBLANES):
        groups.append(jnp.concatenate(
            [ref[lead + (pl.ds((row0 + g * SUBLANES) * nc + c, SUBLANES, stride=nc), slice(None))]
             for c in range(nc)], axis=-1))
    return jnp.concatenate(groups, axis=0)


def _outproj_kernel(a_ref, cv_ref, x_ref, mod_ref, wo_ref, g2_ref, wrt_ref, x1_ref, h2_ref, lg_ref, *, sub):
    half = a_ref.shape[-1]
    ne = lg_ref.shape[1]
    gate1 = mod_ref[0, 2:3, :]
    shift2 = mod_ref[0, 3:4, :]
    scale2 = mod_ref[0, 4:5, :]
    for sb in range(x_ref.shape[1] // sub):
        rows = slice(sb * sub, (sb + 1) * sub)
        mix = (jnp.dot(a_ref[0, rows, :], wo_ref[:half, :], preferred_element_type=F32)
               + jnp.dot(cv_ref[0, rows, :], wo_ref[half:, :], preferred_element_type=F32))
        x1 = x_ref[0, rows, :] + gate1 * mix
        x1_ref[0, rows, :] = x1
        h2 = x1 * lax.rsqrt(jnp.mean(x1 * x1, axis=-1, keepdims=True) + EPS) * g2_ref[...]
        h2 = h2 * (1.0 + scale2) + shift2
        _store_token_tiles(h2_ref, (0,), sb * sub, h2)
        h2_hi = h2.astype(BF16)
        h2_lo = (h2 - h2_hi.astype(F32)).astype(BF16)
        nt = (((1,), (1,)), ((), ()))
        by_hi = lax.dot_general(wrt_ref[...], h2_hi, nt, preferred_element_type=F32)
        by_lo = lax.dot_general(wrt_ref[:ne, :], h2_lo, nt, preferred_element_type=F32)
        lg_ref[0, :, rows] = by_hi[:ne] + by_hi[ne:] + by_lo


def _outproj_call(attn, conv, x, mod_l, w_out_b, g2, w_router_split, tm=1024, sub=256):
    b, s, d = x.shape
    half = attn.shape[-1]
    ne = w_router_split.shape[0] // 2
    return pl.pallas_call(
        functools.partial(_outproj_kernel, sub=sub),
        grid=(b, s // tm),
        in_specs=[
            pl.BlockSpec((1, tm, half), lambda i, j: (i, j, 0)),
            pl.BlockSpec((1, tm, half), lambda i, j: (i, j, 0)),
            pl.BlockSpec((1, tm, d), lambda i, j: (i, j, 0)),
            pl.BlockSpec((1, N_MOD, d), lambda i, j: (i, 0, 0)),
            pl.BlockSpec((2 * half, d), lambda i, j: (0, 0)),
            pl.BlockSpec((1, d), lambda i, j: (0, 0)),
            pl.BlockSpec((2 * ne, d), lambda i, j: (0, 0)),
        ],
        out_specs=[
            pl.BlockSpec((1, tm, d), lambda i, j: (i, j, 0)),
            pl.BlockSpec((1, tm * d // LANES, LANES), lambda i, j: (i, j, 0)),
            pl.BlockSpec((1, ne, tm), lambda i, j: (i, 0, j)),
        ],
        out_shape=[
            jax.ShapeDtypeStruct((b, s, d), F32),
            jax.ShapeDtypeStruct((b, s * d // LANES, LANES), F32),
            jax.ShapeDtypeStruct((b, ne, s), F32),
        ],
        compiler_params=_cparams(("parallel", "parallel"), 48),
        name="out_proj",
    )(attn, conv, x, mod_l, w_out_b, g2, w_router_split)


def _prefix_lanes(xb):
    r, s = xb.shape
    row = lax.broadcasted_iota(jnp.int32, (LANES, LANES), 0)
    col = lax.broadcasted_iota(jnp.int32, (LANES, LANES), 1)
    tri = jnp.where(row <= col, 1.0, 0.0).astype(BF16)
    carry = jnp.zeros((r, 1), F32)
    outs = []
    for blk in range(s // LANES):
        pb = jnp.dot(xb[:, blk * LANES:(blk + 1) * LANES], tri, preferred_element_type=F32) + carry
        outs.append(pb)
        carry = pb[:, LANES - 1:LANES]
    return jnp.concatenate(outs, axis=-1)


def _route_kernel(lg_ref, idx_ref, gate_ref, *, cap, refine):
    lg = lg_ref[0]
    ne, s = lg.shape
    idx_bits = (s - 1).bit_length()
    m = jnp.max(lg, axis=0, keepdims=True)
    ex = jnp.exp(lg - m)
    aff = ex / jnp.sum(ex, axis=0, keepdims=True)

    def count_ge(thr):
        return jnp.sum(jnp.where(aff >= thr, 1.0, 0.0), axis=-1, keepdims=True)

    def enough(bits):
        return count_ge(pltpu.bitcast(bits, F32)) >= cap

    cur = jnp.zeros((ne, 1), jnp.int32)
    for bit in range(30, 0, -2):
        c_hi = cur | (1 << bit)
        c_lo = cur | (1 << (bit - 1))
        c_both = c_hi | (1 << (bit - 1))
        cur = jnp.where(enough(c_both), c_both, jnp.where(enough(c_hi), c_hi, jnp.where(enough(c_lo), c_lo, cur)))
    cur = jnp.where(enough(cur | 1), cur | 1, cur)
    lo = pltpu.bitcast(cur, F32)
    hi = pltpu.bitcast(cur + 1, F32)
    for _ in range(refine):
        t1 = lo + 0.25 * (hi - lo)
        t2 = lo + 0.5 * (hi - lo)
        t3 = lo + 0.75 * (hi - lo)
        e1 = count_ge(t1) >= cap
        e2 = count_ge(t2) >= cap
        e3 = count_ge(t3) >= cap
        lo, hi = (jnp.where(e3, t3, jnp.where(e2, t2, jnp.where(e1, t1, lo))),
                  jnp.where(e1, jnp.where(e2, jnp.where(e3, hi, t3), t2), t1))
    above = aff >= hi
    window = (aff >= lo) & jnp.logical_not(above)
    c_above = _prefix_lanes(jnp.where(above, 1.0, 0.0).astype(BF16))
    c_win = _prefix_lanes(jnp.where(window, 1.0, 0.0).astype(BF16))
    need = cap - c_above[:, -1:]
    sel = above | (window & (c_win <= need))
    rank = (c_above + jnp.minimum(c_win, need)).astype(jnp.int32) - 1
    t = lax.broadcasted_iota(jnp.int32, (ne, s), 1)
    w = jnp.where(sel, ((t - rank) << (idx_bits + 1)) | (t << 1) | 1, 0)
    g = aff
    for kk in range(idx_bits):
        step = 1 << kk
        w_in = pltpu.roll(w, s - step, 1)
        g_in = pltpu.roll(g, s - step, 1)
        take = ((w_in >> (idx_bits + 1 + kk)) & w_in & 1) == 1
        stay = ((w >> (idx_bits + 1 + kk)) & 1) == 0
        w = jnp.where(take, w_in, jnp.where(stay, w, 0))
        g = jnp.where(take, g_in, g)
    idx_ref[0] = (w[:, :cap] >> 1) & ((1 << idx_bits) - 1)
    gate_ref[0] = g[:, :cap]


def _route_call(logits_t, cap, refine=3):
    b, ne, s = logits_t.shape
    return pl.pallas_call(
        functools.partial(_route_kernel, cap=cap, refine=refine),
        grid=(b,),
        in_specs=[pl.BlockSpec((1, ne, s), lambda i: (i, 0, 0))],
        out_specs=[pl.BlockSpec((1, ne, cap), lambda i: (i, 0, 0))] * 2,
        out_shape=[jax.ShapeDtypeStruct((b, ne, cap), jnp.int32), jax.ShapeDtypeStruct((b, ne, cap), F32)],
        compiler_params=_cparams(("parallel",), 32),
        name="route",
    )(logits_t)


def _moe_kernel(idx_ref, gate_ref, h2_ref, wg_ref, wu_ref, wd_ref, acc_ref, xe_ref, ye_ref, *, cap, grp, d):
    e = pl.program_id(1)
    last = pl.num_programs(1) - 1
    nc = d // LANES
    slot = e % 2
    other = 1 - slot
    this_base = e * cap
    prev_base = jnp.maximum(e - 1, 0) * cap
    next_base = jnp.minimum(e + 1, last) * cap

    def token_rows(tok):
        return pl.ds(pl.multiple_of(tok * nc, nc), nc)

    def accumulate(base, y_slot, i):
        toks = [idx_ref[0, 0, base + i * grp + u] for u in range(grp)]
        new = [acc_ref[0, token_rows(toks[u]), :]
               + gate_ref[0, 0, base + i * grp + u] * ye_ref[y_slot, token_rows(i * grp + u), :] for u in range(grp)]
        for u in range(grp):
            acc_ref[0, token_rows(toks[u]), :] = new[u]

    @pl.when(e == 0)
    def _():
        acc_ref[...] = jnp.zeros(acc_ref.shape, F32)
        ye_ref[1] = jnp.zeros(ye_ref.shape[1:], F32)

        def gather(i, carry):
            for u in range(grp):
                j = i * grp + u
                xe_ref[0, token_rows(j), :] = h2_ref[0, token_rows(idx_ref[0, 0, j]), :]
            return carry

        lax.fori_loop(0, cap // grp, gather, 0)

    xe = _load_token_tiles(xe_ref, (slot,), 0, cap, d).astype(BF16)
    for i in range(cap // grp):
        accumulate(prev_base, other, i)
    for j in range(cap):
        xe_ref[other, token_rows(j), :] = h2_ref[0, token_rows(idx_ref[0, 0, next_base + j]), :]
    g = jnp.dot(xe, wg_ref[0, 0], preferred_element_type=F32)
    hid = (g * jax.nn.sigmoid(g)) * jnp.dot(xe, wu_ref[0, 0], preferred_element_type=F32)
    _store_token_tiles(ye_ref, (slot,), 0, jnp.dot(hid.astype(BF16), wd_ref[0, 0], preferred_element_type=F32))

    @pl.when(e == last)
    def _():
        def tail(i, carry):
            accumulate(this_base, slot, i)
            return carry

        lax.fori_loop(0, cap // grp, tail, 0)


def _moe_call(idx, gate, h2_tt, wg_b, wu_b, wd_b, layer, d, grp=8):
    b, rows, _ = h2_tt.shape
    ne, cap = idx.shape[1:]
    ff = wd_b.shape[2]
    nc = d // LANES
    per_sequence = pl.BlockSpec((1, 1, ne * cap), lambda i, e: (i, 0, 0), memory_space=pltpu.SMEM)
    return pl.pallas_call(
        functools.partial(_moe_kernel, cap=cap, grp=grp, d=d),
        grid=(b, ne),
        in_specs=[
            per_sequence, per_sequence,
            pl.BlockSpec((1, rows, LANES), lambda i, e: (i, 0, 0)),
            pl.BlockSpec((1, 1, d, ff), lambda i, e: (layer, e, 0, 0)),
            pl.BlockSpec((1, 1, d, ff), lambda i, e: (layer, e, 0, 0)),
            pl.BlockSpec((1, 1, ff, d), lambda i, e: (layer, e, 0, 0)),
        ],
        out_specs=pl.BlockSpec((1, rows, LANES), lambda i, e: (i, 0, 0)),
        out_shape=jax.ShapeDtypeStruct((b, rows, LANES), F32),
        scratch_shapes=[pltpu.VMEM((2, cap * nc, LANES), F32), pltpu.VMEM((2, cap * nc, LANES), F32)],
        compiler_params=_cparams(("arbitrary", "arbitrary"), 56),
        name="moe_experts",
    )(idx.reshape(b, 1, ne * cap), gate.reshape(b, 1, ne * cap), h2_tt, wg_b, wu_b, wd_b)


def _combine_kernel(x1_ref, acc_ref, mod_ref, o_ref, *, sub):
    gate2 = mod_ref[0, 5:6, :]
    d = x1_ref.shape[-1]
    for sb in range(x1_ref.shape[1] // sub):
        rows = slice(sb * sub, (sb + 1) * sub)
        o_ref[0, rows, :] = x1_ref[0, rows, :] + gate2 * _load_token_tiles(acc_ref, (0,), sb * sub, sub, d)


def _combine_call(x1, acc_tt, mod_l, tm=1024, sub=128):
    b, s, d = x1.shape
    return pl.pallas_call(
        functools.partial(_combine_kernel, sub=sub),
        grid=(b, s // tm),
        in_specs=[
            pl.BlockSpec((1, tm, d), lambda i, j: (i, j, 0)),
            pl.BlockSpec((1, tm * d // LANES, LANES), lambda i, j: (i, j, 0)),
            pl.BlockSpec((1, N_MOD, d), lambda i, j: (i, 0, 0)),
        ],
        out_specs=pl.BlockSpec((1, tm, d), lambda i, j: (i, j, 0)),
        out_shape=jax.ShapeDtypeStruct((b, s, d), F32),
        compiler_params=_cparams(("parallel", "parallel"), 32),
        name="moe_combine",
    )(x1, acc_tt, mod_l)


def _permute_weights(w_in, w_uq, w_ukv):
    o1 = Q_RANK
    o2 = o1 + KV_RANK
    o3 = o2 + ROPE
    w_in_p = jnp.concatenate([w_in[..., :o2], w_in[..., o3:], w_in[..., o2:o3]], axis=-1)
    depth, qr, _ = w_uq.shape
    wq = w_uq.reshape(depth, qr, N_HEADS, QK_DIM)
    w_uq_p = jnp.concatenate([wq[..., :NOPE].reshape(depth, qr, N_HEADS * NOPE),
                              wq[..., NOPE:].reshape(depth, qr, N_HEADS * ROPE)], axis=-1)
    kr = w_ukv.shape[1]
    wkv = w_ukv.reshape(depth, kr, N_HEADS, NOPE + V_DIM)
    w_ukv_p = jnp.concatenate([wkv[..., :NOPE].reshape(depth, kr, N_HEADS * NOPE),
                               wkv[..., NOPE:].reshape(depth, kr, N_HEADS * V_DIM)], axis=-1)
    return w_in_p.astype(BF16), w_uq_p.astype(BF16), w_ukv_p.astype(BF16)


def kernel(x, c, positions, norm1_g, w_ada, b_ada, w_in, q_latent_g, w_uq, kv_latent_g, w_ukv, q_head_g, k_head_g, conv_w, conv_b, conv_norm_g, conv_norm_b, w_out, norm2_g, w_router, w_gate, w_up, w_down):
    b, s, d = x.shape
    depth = w_ada.shape[0]
    cap = 2 * s // N_EXPERTS

    mod = _ada_call(c, w_ada, b_ada).reshape(depth, b, N_MOD, d)
    cos4, sin4 = _rope_tables(positions)
    w_in_p, w_uq_p, w_ukv_p = _permute_weights(w_in, w_uq, w_ukv)
    w_out_b = w_out.astype(BF16)
    wg_b = w_gate.astype(BF16)
    wu_b = w_up.astype(BF16)
    wd_b = w_down.astype(BF16)
    w_router_t = jnp.swapaxes(w_router, 1, 2)
    head = lax.bitcast_convert_type(lax.bitcast_convert_type(w_router_t, jnp.uint32) & jnp.uint32(0xFFFF0000), F32)
    w_router_split = jnp.concatenate([head.astype(BF16), (w_router_t - head).astype(BF16)], axis=1)

    for l in range(depth):
        gqn = q_head_g[l, :NOPE].reshape(1, NOPE)
        gqr2 = jnp.tile(q_head_g[l, NOPE:], 2).reshape(1, LANES)
        gkn = k_head_g[l, :NOPE].reshape(1, NOPE)
        gkr2 = jnp.tile(k_head_g[l, NOPE:], 2).reshape(1, LANES)
        q, k, v, y = _inproj_call(x, mod[l], norm1_g[l].reshape(1, d), w_in_p[l],
                                  q_latent_g[l].reshape(1, Q_RANK), w_uq_p[l],
                                  kv_latent_g[l].reshape(1, KV_RANK), w_ukv_p[l],
                                  gqn, gqr2, gkn, gkr2, cos4, sin4)
        attn = _attn_call(q, k, v)
        conv = _conv_call(y, conv_w[l], conv_b[l].reshape(1, CONV_CH),
                          conv_norm_g[l].reshape(1, CONV_CH), conv_norm_b[l].reshape(1, CONV_CH))
        x1, h2_tt, logits_t = _outproj_call(attn, conv, x, mod[l], w_out_b[l],
                                            norm2_g[l].reshape(1, d), w_router_split[l])
        idx, gate = _route_call(logits_t, cap)
        acc_tt = _moe_call(idx, gate, h2_tt, wg_b, wu_b, wd_b, l, d)
        x = _combine_call(x1, acc_tt, mod[l])
    return x
```
